```python
import jax
import jax.numpy as jnp
from jax import lax
import numpy as np

D_MODEL = 1024
BATCH = 32
SEQ = 2048
DEPTH = 1

D_MIX = D_MODEL
ATTN_HEADS = 8
ATTN_HEAD_DIM = 64
ATTN_WIDTH = ATTN_HEADS * ATTN_HEAD_DIM
DILATED_PATTERNS = ((128, 1), (512, 4), (2048, 16))
RWKV_HEADS = 8
RWKV_HEAD_DIM = 64
RWKV_WIDTH = RWKV_HEADS * RWKV_HEAD_DIM
DECAY_LORA = 64
ICLR_LORA = 64
GATE_LORA = 128
RWKV_IN_WIDTH = 3 * RWKV_WIDTH + DECAY_LORA + ICLR_LORA + GATE_LORA
IN_WIDTH = 3 * ATTN_WIDTH + RWKV_IN_WIDTH
N_EXPERTS = 32
TOP_K = 4
D_FF_EXPERT = D_MODEL
SWIGLU_ALPHA = 1.702
SWIGLU_LIMIT = 7.0
MOE_BLOCK = 512
NORM_EPS = 1e-5
GROUPNORM_EPS = 64e-5
L2_EPS = 1e-12

kernel_name = 'hybrid_dilated_attn_rwkv7_moe_block'


def rms_norm(x, gain):
    xf = x.astype(jnp.float32)
    y = xf * lax.rsqrt(jnp.mean(xf * xf, axis=-1, keepdims=True) + NORM_EPS)
    return (y * gain.astype(jnp.float32)).astype(x.dtype)


def token_shift(z):
    return jnp.pad(z[:, :-1], ((0, 0), (1, 0), (0, 0)))


def dilated_window_attention(q, k, v, window, dilation):
    b, s, h, e = q.shape
    blk = window // dilation
    m = s // dilation
    nb = -(-m // blk)
    mp = nb * blk

    def to_blocks(t):
        t = t.reshape(b, m, dilation, h, e)
        t = jnp.pad(t, ((0, 0), (0, mp - m), (0, 0), (0, 0), (0, 0)))
        return t.reshape(b, nb, blk, dilation, h, e)

    def with_prev(t):
        prev = jnp.pad(t[:, :-1], ((0, 0), (1, 0), (0, 0), (0, 0), (0, 0), (0, 0)))
        return jnp.concatenate([prev, t], axis=2)

    qb = to_blocks(q)
    kc = with_prev(to_blocks(k))
    vc = with_prev(to_blocks(v))
    scores = jnp.einsum('bnidhe,bnjdhe->bndhij', qb, kc).astype(jnp.float32) * (e ** -0.5)
    i = jnp.arange(blk)[:, None]
    j = jnp.arange(2 * blk)[None, :]
    dist = blk + i - j
    band = (dist >= 0) & (dist <= blk)
    has_prev = (jnp.arange(nb) > 0)[:, None, None] | (j >= blk)[None]
    mask = band[None] & has_prev
    scores = jnp.where(mask[:, None, None], scores, -jnp.inf)
    lse = jax.nn.logsumexp(scores, axis=-1)
    probs = jnp.exp(scores - lse[..., None]).astype(v.dtype)
    out = jnp.einsum('bndhij,bnjdhe->bnidhe', probs, vc)
    out = out.reshape(b, mp, dilation, h, e)[:, :m].reshape(b, s, h, e)
    lse = lse.transpose(0, 1, 4, 2, 3).reshape(b, mp, dilation, h)[:, :m].reshape(b, s, h)
    return out, lse


def dilated_attention_group(z_attn, out_gain):
    b, s, _ = z_attn.shape
    q, k, v = jnp.split(z_attn, 3, axis=-1)
    q = q.reshape(b, s, ATTN_HEADS, ATTN_HEAD_DIM)
    k = k.reshape(b, s, ATTN_HEADS, ATTN_HEAD_DIM)
    v = v.reshape(b, s, ATTN_HEADS, ATTN_HEAD_DIM)
    outs, lses = [], []
    for window, dilation in DILATED_PATTERNS:
        o, l = dilated_window_attention(q, k, v, window, dilation)
        outs.append(o.astype(jnp.float32))
        lses.append(l)
    wts = jax.nn.softmax(jnp.stack(lses, 0), axis=0)
    o = jnp.sum(wts[..., None] * jnp.stack(outs, 0), axis=0)
    o = o * lax.rsqrt(jnp.mean(o * o, axis=-1, keepdims=True) + NORM_EPS)
    return (o.reshape(b, s, ATTN_WIDTH) * out_gain.astype(jnp.float32)).astype(z_attn.dtype)


def wkv7_scan(r, decay, k, v, a, bb):
    bsz, _, h, n = r.shape

    def step(state, inp):
        r_t, w_t, k_t, v_t, a_t, b_t = inp
        sa = jnp.einsum('bhij,bhj->bhi', state, a_t)
        state = (state * w_t[:, :, None, :] + sa[..., None] * b_t[:, :, None, :]
                 + v_t[..., None] * k_t[:, :, None, :])
        return state, jnp.einsum('bhij,bhj->bhi', state, r_t)

    xs = tuple(jnp.swapaxes(t, 0, 1) for t in (r, decay, k, v, a, bb))
    state0 = jnp.zeros((bsz, h, n, n), jnp.float32)
    _, ys = lax.scan(step, state0, xs)
    return jnp.swapaxes(ys, 0, 1)


def rwkv7_group(z_rwkv, shift_mu, decay_up, decay_bias, iclr_up, iclr_bias, gate_up,
                k_k, k_a, r_k, lnx_w, lnx_b):
    f32 = jnp.float32
    b, s, _ = z_rwkv.shape
    hn = (RWKV_HEADS, RWKV_HEAD_DIM)
    zs = z_rwkv + (token_shift(z_rwkv) - z_rwkv) * shift_mu
    c0 = RWKV_WIDTH
    c1 = 2 * RWKV_WIDTH
    c2 = 3 * RWKV_WIDTH
    c3 = c2 + DECAY_LORA
    c4 = c3 + ICLR_LORA
    r, k, v, wd, ad, gd = jnp.split(zs, [c0, c1, c2, c3, c4], axis=-1)
    w = -jax.nn.softplus(-(decay_bias + jnp.tanh(wd) @ decay_up).astype(f32)) - 0.5
    decay = jnp.exp(-jnp.exp(w))
    a = jax.nn.sigmoid((iclr_bias + ad @ iclr_up).astype(f32))
    g = (jax.nn.sigmoid(gd) @ gate_up).astype(f32)

    def heads(t):
        return t.astype(f32).reshape(b, s, RWKV_HEADS, RWKV_HEAD_DIM)

    r, k, v, a, decay = heads(r), heads(k), heads(v), heads(a), heads(decay)
    kk = k * k_k.astype(f32).reshape(hn)
    kk = kk / jnp.maximum(jnp.linalg.norm(kk, axis=-1, keepdims=True), L2_EPS)
    k = k * (1.0 + (a - 1.0) * k_a.astype(f32).reshape(hn))
    y = wkv7_scan(r, decay, k, v, -kk, kk * a)
    mean = jnp.mean(y, axis=-1, keepdims=True)
    var = jnp.mean(jnp.square(y - mean), axis=-1, keepdims=True)
    y = ((y - mean) * lax.rsqrt(var + GROUPNORM_EPS) * lnx_w.astype(f32).reshape(hn)
         + lnx_b.astype(f32).reshape(hn))
    y = y + jnp.sum(r * k * r_k.astype(f32), axis=-1, keepdims=True) * v
    return (y.reshape(b, s, RWKV_WIDTH) * g).astype(z_rwkv.dtype)


def clamped_swiglu(u):
    glu, lin = jnp.split(u, 2, axis=-1)
    glu = jnp.minimum(glu, SWIGLU_LIMIT)
    lin = jnp.clip(lin, -SWIGLU_LIMIT, SWIGLU_LIMIT)
    return glu * jax.nn.sigmoid(SWIGLU_ALPHA * glu) * (lin + 1.0)


def moe_ffn(h, router_w, router_b, w1, b1, w2, b2):
    n, d = h.shape
    nk = n * TOP_K
    logits = (h @ router_w + router_b).astype(jnp.float32)
    top_vals, top_idx = lax.top_k(logits, TOP_K)
    gates = jax.nn.softmax(top_vals, axis=-1).reshape(nk)
    expert_of = top_idx.reshape(nk)
    token_of = jnp.arange(nk, dtype=jnp.int32) // TOP_K
    order = jnp.argsort(expert_of)
    e_sorted = expert_of[order]
    sizes = jnp.bincount(expert_of, length=N_EXPERTS)
    starts = jnp.cumsum(sizes) - sizes
    padded = (sizes + MOE_BLOCK - 1) // MOE_BLOCK * MOE_BLOCK
    pad_ends = jnp.cumsum(padded)
    dest = pad_ends[e_sorted] - padded[e_sorted] + jnp.arange(nk) - starts[e_sorted]
    n_blocks = -(-nk // MOE_BLOCK) + N_EXPERTS
    rows = n_blocks * MOE_BLOCK
    slot_token = jnp.zeros((rows,), jnp.int32).at[dest].set(token_of[order])
    slot_gate = jnp.zeros((rows,), jnp.float32).at[dest].set(gates[order])
    block_expert = jnp.minimum(
        jnp.searchsorted(pad_ends, jnp.arange(n_blocks) * MOE_BLOCK, side='right'), N_EXPERTS - 1)

    def expert_block(args):
        tok, gate, e = args
        xb = h[tok]
        u = xb @ w1[e] + b1[e]
        yb = clamped_swiglu(u) @ w2[e] + b2[e]
        return yb * gate[:, None].astype(yb.dtype)

    y = lax.map(expert_block, (slot_token.reshape(n_blocks, MOE_BLOCK),
                               slot_gate.reshape(n_blocks, MOE_BLOCK), block_expert))
    return jax.ops.segment_sum(y.reshape(rows, d), slot_token, num_segments=n)


def setup_inputs(seed: int = 0) -> dict:
    key = jax.random.key(seed)
    ks = jax.random.split(key, 24)
    f32 = jnp.float32

    def normal(k, shape, scale):
        return jax.random.normal(k, shape, f32) * scale

    def gain(k, shape):
        return 1.0 + 0.02 * jax.random.normal(k, shape, f32)

    L, D, E, F = DEPTH, D_MODEL, N_EXPERTS, D_FF_EXPERT
    return {
        'x': normal(ks[0], (BATCH, SEQ, D), 1.0),
        'norm1_gain': gain(ks[1], (L, D)),
        'w_in': normal(ks[2], (L, D, IN_WIDTH), D ** -0.5),
        'shift_mu': jax.random.uniform(ks[3], (L, RWKV_IN_WIDTH), f32),
        'decay_up': normal(ks[4], (L, DECAY_LORA, RWKV_WIDTH), 0.1),
        'decay_bias': jax.random.uniform(ks[5], (L, RWKV_WIDTH), f32, -4.0, 1.0),
        'iclr_up': normal(ks[6], (L, ICLR_LORA, RWKV_WIDTH), 0.1),
        'iclr_bias': normal(ks[7], (L, RWKV_WIDTH), 0.1),
        'gate_up': normal(ks[8], (L, GATE_LORA, RWKV_WIDTH), GATE_LORA ** -0.5),
        'k_k': 0.85 + normal(ks[9], (L, RWKV_WIDTH), 0.05),
        'k_a': 1.0 + normal(ks[10], (L, RWKV_WIDTH), 0.05),
        'r_k': normal(ks[11], (L, RWKV_HEADS, RWKV_HEAD_DIM), 0.1),
        'lnx_w': gain(ks[12], (L, RWKV_WIDTH)),
        'lnx_b': normal(ks[13], (L, RWKV_WIDTH), 0.02),
        'attn_out_gain': gain(ks[14], (L, ATTN_WIDTH)),
        'w_out': normal(ks[15], (L, D_MIX, D), D_MIX ** -0.5),
        'norm2_gain': gain(ks[16], (L, D)),
        'router_w': normal(ks[17], (L, D, E), D ** -0.5),
        'router_b': normal(ks[18], (L, E), 0.01),
        'expert_w1': normal(ks[19], (L, E, D, 2 * F), D ** -0.5),
        'expert_b1': normal(ks[20], (L, E, 2 * F), 0.01),
        'expert_w2': normal(ks[21], (L, E, F, D), F ** -0.5),
        'expert_b2': normal(ks[22], (L, E, D), 0.01),
        'final_norm_gain': gain(ks[23], (D,)),
    }


def reference(x, norm1_gain, w_in, shift_mu, decay_up, decay_bias, iclr_up, iclr_bias, gate_up,
              k_k, k_a, r_k, lnx_w, lnx_b, attn_out_gain, w_out, norm2_gain, router_w, router_b,
              expert_w1, expert_b1, expert_w2, expert_b2, final_norm_gain):
    b, s, d = x.shape
    for l in range(DEPTH):
        h = rms_norm(x, norm1_gain[l])
        z = h @ w_in[l]
        z_attn = z[..., :3 * ATTN_WIDTH]
        z_rwkv = z[..., 3 * ATTN_WIDTH:]
        y_attn = dilated_attention_group(z_attn, attn_out_gain[l])
        y_rwkv = rwkv7_group(z_rwkv, shift_mu[l], decay_up[l], decay_bias[l], iclr_up[l],
                             iclr_bias[l], gate_up[l], k_k[l], k_a[l], r_k[l], lnx_w[l], lnx_b[l])
        x = x + jnp.concatenate([y_attn, y_rwkv], axis=-1) @ w_out[l]
        h = rms_norm(x, norm2_gain[l]).reshape(b * s, d)
        x = x + moe_ffn(h, router_w[l], router_b[l], expert_w1[l], expert_b1[l],
                        expert_w2[l], expert_b2[l]).reshape(b, s, d).astype(x.dtype)
    return rms_norm(x, final_norm_gain)
```

```python
import functools

import jax
import jax.numpy as jnp
from jax import lax
from jax.experimental import pallas as pl
from jax.experimental.pallas import tpu as pltpu

F32 = jnp.float32
BF16 = jnp.bfloat16
I32 = jnp.int32

D_MODEL = 1024
HEADS = 8
HEAD_DIM = 64
ATTN_WIDTH = HEADS * HEAD_DIM
RWKV_WIDTH = HEADS * HEAD_DIM
DECAY_LORA = 64
ICLR_LORA = 64
GATE_LORA = 128
RWKV_IN = 3 * RWKV_WIDTH + DECAY_LORA + ICLR_LORA + GATE_LORA
IN_WIDTH = 3 * ATTN_WIDTH + RWKV_IN
N_EXPERTS = 32
TOP_K = 4
SWIGLU_ALPHA = 1.702
SWIGLU_LIMIT = 7.0
NORM_EPS = 1e-5
GROUPNORM_EPS = 64e-5
L2_EPS = 1e-12

LANES = 128
SUBLANES = 8
VMEM_LIMIT = 56 * 1024 * 1024
NEG_INF = float("-inf")


def _dot(a, b):
    return jnp.dot(a, b, preferred_element_type=F32)


def _dot_nt(a, b):
    return lax.dot_general(a, b, (((1,), (1,)), ((), ())), preferred_element_type=F32)


def _params(sem, vmem=VMEM_LIMIT):
    return pltpu.CompilerParams(dimension_semantics=sem, vmem_limit_bytes=vmem)


def _inproj_kernel(x_ref, g_ref, w_ref, za_ref, zr_ref):
    x = x_ref[...]
    ms = jnp.mean(x * x, axis=-1, keepdims=True)
    h = (x * lax.rsqrt(ms + NORM_EPS) * g_ref[...]).astype(BF16)
    na = za_ref.shape[1]
    za_ref[...] = _dot(h, w_ref[:, :na])
    zr_ref[...] = _dot(h, w_ref[:, na:])


def _inproj(x2, gain, w_in_bf16, tm=512):
    n, d = x2.shape
    na = 3 * ATTN_WIDTH
    return pl.pallas_call(
        _inproj_kernel,
        grid=(n // tm,),
        in_specs=[
            pl.BlockSpec((tm, d), lambda i: (i, 0)),
            pl.BlockSpec((1, d), lambda i: (0, 0)),
            pl.BlockSpec((d, IN_WIDTH), lambda i: (0, 0)),
        ],
        out_specs=[
            pl.BlockSpec((tm, na), lambda i: (i, 0)),
            pl.BlockSpec((tm, RWKV_IN), lambda i: (i, 0)),
        ],
        out_shape=[
            jax.ShapeDtypeStruct((n, na), F32),
            jax.ShapeDtypeStruct((n, RWKV_IN), F32),
        ],
        compiler_params=_params(("parallel",)),
        name="inproj",
    )(x2, gain.reshape(1, d), w_in_bf16)


ATT_BLK = 128
ATT_DIL = 16


def _attn_kernel(q_ref, k_ref, v_ref, g_ref, o_ref,
                 kp_ref, vp_ref, m1_ref, l1_ref, a1_ref, of_ref):
    seq = q_ref.shape[0]
    nblk = seq // ATT_BLK
    scale = HEAD_DIM ** -0.5
    lane = lax.broadcasted_iota(I32, (1, LANES), 1)
    head_masks = (lane < HEAD_DIM, lane >= HEAD_DIM)

    for r in range(ATT_DIL):
        pb = ((r % 4) * 4 + r // 4) * ATT_BLK
        kp_ref[pb:pb + ATT_BLK, :] = k_ref[pl.ds(r, ATT_BLK, stride=ATT_DIL), :].astype(BF16)
        vp_ref[pb:pb + ATT_BLK, :] = v_ref[pl.ds(r, ATT_BLK, stride=ATT_DIL), :].astype(BF16)

    ii = lax.broadcasted_iota(I32, (ATT_BLK, ATT_BLK), 0)
    jj = lax.broadcasted_iota(I32, (ATT_BLK, ATT_BLK), 1)

    def pack_heads(vals):
        return jnp.where(head_masks[0], vals[0], vals[1])

    def blk_body(b, carry):
        r0 = pl.multiple_of(b * ATT_BLK, ATT_BLK)
        p0 = pl.multiple_of(jnp.maximum(b - 1, 0) * ATT_BLK, ATT_BLK)
        qb = q_ref[pl.ds(r0, ATT_BLK), :] * scale
        kc = k_ref[pl.ds(r0, ATT_BLK), :].astype(BF16)
        kq = k_ref[pl.ds(p0, ATT_BLK), :].astype(BF16)
        vc = v_ref[pl.ds(r0, ATT_BLK), :].astype(BF16)
        vq = v_ref[pl.ds(p0, ATT_BLK), :].astype(BF16)
        cur_ok = jj <= ii
        prev_ok = (jj >= ii) & (b > 0)
        ms, ls, accs = [], [], []
        for hm in head_masks:
            qh = jnp.where(hm, qb, 0.0).astype(BF16)
            sc = jnp.where(cur_ok, _dot_nt(qh, kc), NEG_INF)
            sp = jnp.where(prev_ok, _dot_nt(qh, kq), NEG_INF)
            m = jnp.maximum(jnp.max(sc, axis=-1, keepdims=True), jnp.max(sp, axis=-1, keepdims=True))
            pc = jnp.exp(sc - m)
            pp = jnp.exp(sp - m)
            ls.append(jnp.sum(pc, axis=-1, keepdims=True) + jnp.sum(pp, axis=-1, keepdims=True))
            accs.append(_dot(pc.astype(BF16), vc) + _dot(pp.astype(BF16), vq))
            ms.append(m)
        m1_ref[pl.ds(r0, ATT_BLK), :] = pack_heads(ms)
        l1_ref[pl.ds(r0, ATT_BLK), :] = pack_heads(ls)
        a1_ref[pl.ds(r0, ATT_BLK), :] = pack_heads(accs)
        return carry

    lax.fori_loop(0, nblk, blk_body, 0)

    nk = 4 * ATT_BLK
    qi = lax.broadcasted_iota(I32, (ATT_BLK, nk), 0)
    kc_i = lax.broadcasted_iota(I32, (ATT_BLK, nk), 1)
    k_hi = kc_i >> 7
    k_pos = kc_i & (ATT_BLK - 1)
    gain = g_ref[...]

    def res_body(r, carry):
        r_lo = r & 3
        r_hi = r >> 2
        k0 = pl.multiple_of(r_lo * nk, nk)
        qr = q_ref[pl.ds(r, ATT_BLK, stride=ATT_DIL), :] * scale
        kc = kp_ref[pl.ds(k0, nk), :]
        vc = vp_ref[pl.ds(k0, nk), :]
        delta = ATT_DIL * (qi - k_pos) + 4 * (r_hi - k_hi)
        in4 = (delta >= 0) & (delta <= 4 * ATT_BLK)
        in16 = (k_hi == r_hi) & (delta >= 0)
        cnt = in4.astype(F32) + in16.astype(F32)
        seen = in4 | in16
        ms, ls, accs = [], [], []
        for hm in head_masks:
            qh = jnp.where(hm, qr, 0.0).astype(BF16)
            s = jnp.where(seen, _dot_nt(qh, kc), NEG_INF)
            m = jnp.max(s, axis=-1, keepdims=True)
            e = jnp.exp(s - m) * cnt
            ls.append(jnp.sum(e, axis=-1, keepdims=True))
            accs.append(_dot(e.astype(BF16), vc))
            ms.append(m)
        m2, l2, a2 = pack_heads(ms), pack_heads(ls), pack_heads(accs)
        rows = pl.ds(r, ATT_BLK, stride=ATT_DIL)
        m1 = m1_ref[rows, :]
        m = jnp.maximum(m1, m2)
        w1 = jnp.exp(m1 - m)
        w2 = jnp.exp(m2 - m)
        den = l1_ref[rows, :] * w1 + l2 * w2
        o = (a1_ref[rows, :] * w1 + a2 * w2) / den
        o2 = o * o
        msq = pack_heads([jnp.sum(jnp.where(hm, o2, 0.0), axis=-1, keepdims=True) for hm in head_masks])
        of_ref[rows, :] = o * lax.rsqrt(msq * (1.0 / HEAD_DIM) + NORM_EPS) * gain
        return carry

    lax.fori_loop(0, ATT_DIL, res_body, 0)
    o_ref[...] = of_ref[...].astype(o_ref.dtype)


def _attention(z_attn, out_gain, batch, seq):
    assert seq == ATT_BLK * ATT_DIL
    pairs = ATTN_WIDTH // LANES
    z3 = z_attn.reshape(batch, seq, 3 * ATTN_WIDTH)
    blk = lambda off: pl.BlockSpec((None, seq, LANES), lambda b, p: (b, 0, off + p))
    out = pl.pallas_call(
        _attn_kernel,
        grid=(batch, pairs),
        in_specs=[blk(0), blk(pairs), blk(2 * pairs),
                  pl.BlockSpec((1, LANES), lambda b, p: (0, p))],
        out_specs=pl.BlockSpec((None, seq, LANES), lambda b, p: (b, 0, p)),
        out_shape=jax.ShapeDtypeStruct((batch, seq, ATTN_WIDTH), BF16),
        scratch_shapes=[
            pltpu.VMEM((seq, LANES), BF16), pltpu.VMEM((seq, LANES), BF16),
            pltpu.VMEM((seq, LANES), F32), pltpu.VMEM((seq, LANES), F32),
            pltpu.VMEM((seq, LANES), F32), pltpu.VMEM((seq, LANES), F32),
        ],
        compiler_params=_params(("parallel", "parallel")),
        name="dilated_attn",
    )(z3, z3, z3, out_gain.reshape(1, ATTN_WIDTH))
    return out.reshape(batch * seq, ATTN_WIDTH)


WKV_CHUNK = 64


def _split3(x):
    h = x.astype(BF16)
    r1 = x - h.astype(F32)
    m = r1.astype(BF16)
    l = (r1 - m.astype(F32)).astype(BF16)
    return h, m, l


def _seg_sum(x, seg):
    hi = x.astype(BF16)
    lo = (x - hi.astype(F32)).astype(BF16)
    outs = []
    for p in range(x.shape[1] // LANES):
        sl = slice(p * LANES, (p + 1) * LANES)
        outs.append(_dot(hi[:, sl], seg) + _dot(lo[:, sl], seg))
    return jnp.concatenate(outs, axis=1)


def _softplus(y):
    return jnp.maximum(y, 0.0) + jnp.log(1.0 + jnp.exp(-jnp.abs(y)))


def _sigmoid(y):
    return 1.0 / (1.0 + jnp.exp(-y))


def _rwkv_kernel(z_ref, mu_ref, wl_ref, db_ref, ib_ref, gu_ref, kk_ref, ka_ref, rk_ref,
                 lnw_ref, lnb_ref, o_ref, st_ref, prev_ref):
    cs = z_ref.shape[0]
    w = RWKV_WIDTH
    pairs = w // LANES

    @pl.when(pl.program_id(1) == 0)
    def _():
        st_ref[...] = jnp.zeros_like(st_ref)
        prev_ref[...] = jnp.zeros_like(prev_ref)

    z = z_ref[...]
    row = lax.broadcasted_iota(I32, (cs, 1), 0)
    zprev = jnp.where(row == 0, prev_ref[0:1, :], pltpu.roll(z, 1, axis=0))
    prev_ref[0:1, :] = z[cs - 1:cs, :]
    zs = z + (zprev - z) * mu_ref[...]

    r_ = zs[:, 0:w]
    k_ = zs[:, w:2 * w]
    v_ = zs[:, 2 * w:3 * w]
    lora_in = zs[:, 3 * w:3 * w + LANES]
    gate_in = zs[:, 3 * w + LANES:]

    lane = lax.broadcasted_iota(I32, (1, LANES), 1)
    head_a = lane < HEAD_DIM
    xl = jnp.where(head_a, jnp.tanh(lora_in), lora_in).astype(BF16)
    pre = _dot(xl, wl_ref[...])
    wlog = -_softplus(-(pre[:, :w] + db_ref[...])) - 0.5
    lw = -jnp.exp(wlog)
    rate = _sigmoid(pre[:, w:] + ib_ref[...])
    gate = _dot(_sigmoid(gate_in).astype(BF16), gu_ref[...])

    li = lax.broadcasted_iota(I32, (LANES, LANES), 0)
    lj = lax.broadcasted_iota(I32, (LANES, LANES), 1)
    seg = ((li >> 6) == (lj >> 6)).astype(BF16)

    kk = k_ * kk_ref[...]
    kk = kk / jnp.maximum(jnp.sqrt(_seg_sum(kk * kk, seg)), L2_EPS)
    kp = k_ * (1.0 + (rate - 1.0) * ka_ref[...])
    an = -kk
    bn = kk * rate

    ti = lax.broadcasted_iota(I32, (cs, cs), 0)
    tj = lax.broadcasted_iota(I32, (cs, cs), 1)
    tri = (ti >= tj).astype(BF16)
    cum = sum(_dot(tri, t) for t in _split3(lw))
    cend = cum[cs - 1:cs, :]
    g_end = jnp.exp(cend)
    at = an * jnp.exp(cum - lw)
    rt = r_ * jnp.exp(cum)
    e_inv = jnp.exp(-cum)
    bt = bn * e_inv
    kt = kp * e_inv
    e_end = jnp.exp(cend - cum)
    bp = bn * e_end
    kp_end = kp * e_end

    n2 = 2 * cs
    assert n2 == LANES
    strict = li < lj
    incl = li <= lj
    eye = (li == lj).astype(F32)
    eye_b = eye.astype(BF16)
    sel = ((lax.broadcasted_iota(I32, (cs, LANES), 1) & (cs - 1))
           == lax.broadcasted_iota(I32, (cs, LANES), 0)).astype(BF16)

    def stack2(x):
        return jnp.concatenate([jnp.where(head_a, x, 0.0), jnp.where(head_a, 0.0, x)], axis=0).astype(BF16)

    ys = []
    for p in range(pairs):
        sl = slice(p * LANES, (p + 1) * LANES)
        a_s, r_s = stack2(at[:, sl]), stack2(rt[:, sl])
        b_s, k_s = stack2(bt[:, sl]), stack2(kt[:, sl])
        bp_s, kp_s = stack2(bp[:, sl]), stack2(kp_end[:, sl])
        v_s = stack2(v_[:, sl])
        ar = jnp.concatenate([a_s, r_s], axis=0)
        g1 = _dot_nt(jnp.concatenate([b_s, k_s], axis=0), ar)
        ab_t = jnp.where(strict, g1[:n2, :n2], 0.0)
        rb_t = jnp.where(incl, g1[:n2, n2:], 0.0)
        ak_t = jnp.where(strict, g1[n2:, :n2], 0.0)
        rk_t = jnp.where(incl, g1[n2:, n2:], 0.0)

        tt = eye + jnp.where((li >> 1) == (lj >> 1), ab_t, 0.0)
        s = 2
        while s < cs:
            sh = s.bit_length() - 1
            off = ((li >> (sh + 1)) == (lj >> (sh + 1))) & ((li >> sh) != (lj >> sh))
            a_off = jnp.where(off, ab_t, 0.0).astype(BF16)
            tb = tt.astype(BF16)
            tt = tt + _dot(_dot(tb, a_off).astype(BF16), tb)
            s *= 2

        v_t = _dot_nt(eye_b, v_s).astype(BF16)
        s0 = st_ref[p]
        g2 = _dot_nt(s0.astype(BF16), ar)
        g3 = _dot(v_t, jnp.concatenate([ak_t, rk_t], axis=1).astype(BF16))
        u_t = _dot((g2[:, :n2] + g3[:, :n2]).astype(BF16), tt.astype(BF16))
        y_t = g2[:, n2:] + g3[:, n2:] + _dot(u_t.astype(BF16), rb_t.astype(BF16))
        st_ref[p] = s0 * g_end[:, sl] + _dot(
            jnp.concatenate([u_t.astype(BF16), v_t], axis=1),
            jnp.concatenate([bp_s, kp_s], axis=0))
        ys.append(_dot_nt(sel, y_t.astype(BF16)))
    y = jnp.concatenate(ys, axis=1)

    inv_n = 1.0 / HEAD_DIM
    mean = _seg_sum(y, seg) * inv_n
    yc = y - mean
    var = _seg_sum(yc * yc, seg) * inv_n
    y = yc * lax.rsqrt(var + GROUPNORM_EPS) * lnw_ref[...] + lnb_ref[...]
    y = y + _seg_sum(r_ * kp * rk_ref[...], seg) * v_
    o_ref[...] = (y * gate).astype(o_ref.dtype)


def _rwkv(z_rwkv, batch, seq, shift_mu, decay_up, decay_bias, iclr_up, iclr_bias, gate_up,
          k_k, k_a, r_k, lnx_w, lnx_b):
    cs = WKV_CHUNK
    w = RWKV_WIDTH
    z3 = z_rwkv.reshape(batch, seq, RWKV_IN)
    wl = jnp.zeros((LANES, 2 * w), F32)
    wl = wl.at[:DECAY_LORA, :w].set(decay_up).at[DECAY_LORA:, w:].set(iclr_up).astype(BF16)
    row = lambda a: a.reshape(1, -1).astype(F32)
    full = lambda shape: pl.BlockSpec(shape, lambda b, c: (0,) * len(shape))
    out = pl.pallas_call(
        _rwkv_kernel,
        grid=(batch, seq // cs),
        in_specs=[
            pl.BlockSpec((None, cs, RWKV_IN), lambda b, c: (b, c, 0)),
            full((1, RWKV_IN)), full((LANES, 2 * w)), full((1, w)), full((1, w)),
            full((GATE_LORA, w)), full((1, w)), full((1, w)), full((1, w)), full((1, w)), full((1, w)),
        ],
        out_specs=pl.BlockSpec((None, cs, w), lambda b, c: (b, c, 0)),
        out_shape=jax.ShapeDtypeStruct((batch, seq, w), BF16),
        scratch_shapes=[
            pltpu.VMEM((w // LANES, LANES, LANES), F32),
            pltpu.VMEM((SUBLANES, RWKV_IN), F32),
        ],
        compiler_params=_params(("parallel", "arbitrary")),
        name="rwkv7",
    )(z3, row(shift_mu), wl, row(decay_bias), row(iclr_bias), gate_up.astype(BF16),
      row(k_k), row(k_a), row(r_k), row(lnx_w), row(lnx_b))
    return out.reshape(batch * seq, w)


ROUTE_TILE = 256
RUN_ALIGN = SUBLANES
TILE_ROWS = 1280
EXPERT_BLOCK = 512
assert TILE_ROWS >= TOP_K * ROUTE_TILE + N_EXPERTS * (RUN_ALIGN - 1) and TILE_ROWS % LANES == 0
HIGHEST = lax.Precision.HIGHEST


def _router_kernel(ya_ref, yr_ref, x_ref, wo_ref, g_ref, rw_ref, rb_ref,
                   x1_ref, h_ref, route_ref, plen_ref):
    t = x_ref.shape[0]
    half = ya_ref.shape[1]
    x1 = x_ref[...] + _dot(ya_ref[...], wo_ref[:half, :]) + _dot(yr_ref[...], wo_ref[half:, :])
    x1_ref[...] = x1
    ms = jnp.mean(x1 * x1, axis=-1, keepdims=True)
    h = x1 * lax.rsqrt(ms + NORM_EPS) * g_ref[...]
    h_ref[...] = h.astype(BF16)

    lane = lax.broadcasted_iota(I32, (t, LANES), 1)
    lane_f = lane.astype(F32)
    logits = jnp.dot(h, rw_ref[...], precision=HIGHEST, preferred_element_type=F32) + rb_ref[...]
    lg = jnp.where(lane < N_EXPERTS, logits, NEG_INF)
    vals, hots = [], []
    for _ in range(TOP_K):
        m = jnp.max(lg, axis=-1, keepdims=True)
        idx = jnp.min(jnp.where(lg == m, lane_f, float(LANES)), axis=-1, keepdims=True)
        hot = lane_f == idx
        vals.append(m)
        hots.append(hot)
        lg = jnp.where(hot, NEG_INF, lg)
    exps = [jnp.exp(v - vals[0]) for v in vals]
    den = exps[0] + exps[1] + exps[2] + exps[3]

    multi = sum(hh.astype(F32) for hh in hots)
    ti = lax.broadcasted_iota(I32, (t, t), 0)
    tj = lax.broadcasted_iota(I32, (t, t), 1)
    before = _dot((tj < ti).astype(BF16), multi.astype(BF16))
    counts = jnp.sum(multi, axis=0, keepdims=True)
    padded = jnp.ceil(counts * (1.0 / RUN_ALIGN)) * RUN_ALIGN
    li = lax.broadcasted_iota(I32, (LANES, LANES), 0)
    lj = lax.broadcasted_iota(I32, (LANES, LANES), 1)
    run_start = jnp.dot(jnp.broadcast_to(padded, (SUBLANES, LANES)), (li < lj).astype(F32),
                        precision=HIGHEST, preferred_element_type=F32)[0:1, :]
    pos = run_start + before

    route = jnp.zeros((t, LANES), F32)
    for k in range(TOP_K):
        idx_f = jnp.sum(jnp.where(hots[k], lane_f, 0.0), axis=-1, keepdims=True)
        pos_k = jnp.sum(jnp.where(hots[k], pos, 0.0), axis=-1, keepdims=True)
        route = route + jnp.where(lane == k, idx_f, 0.0)
        route = route + jnp.where(lane == TOP_K + k, exps[k] / den, 0.0)
        route = route + jnp.where(lane == 2 * TOP_K + k, pos_k, 0.0)
    route_ref[...] = route
    plen_ref[...] = jnp.broadcast_to(padded, (SUBLANES, LANES))


def _outproj_router(y_attn, y_rwkv, x2, w_out_bf16, gain, router_w, router_b):
    n, d = x2.shape
    t = ROUTE_TILE
    nt = n // t
    half = y_attn.shape[1]
    rw = jnp.zeros((d, LANES), F32).at[:, :N_EXPERTS].set(router_w)
    rb = jnp.zeros((1, LANES), F32).at[0, :N_EXPERTS].set(router_b)
    full = lambda shape: pl.BlockSpec(shape, lambda i: (0,) * len(shape))
    return pl.pallas_call(
        _router_kernel,
        grid=(nt,),
        in_specs=[
            pl.BlockSpec((t, half), lambda i: (i, 0)),
            pl.BlockSpec((t, half), lambda i: (i, 0)),
            pl.BlockSpec((t, d), lambda i: (i, 0)),
            full((2 * half, d)), full((1, d)), full((d, LANES)), full((1, LANES)),
        ],
        out_specs=[
            pl.BlockSpec((t, d), lambda i: (i, 0)),
            pl.BlockSpec((t, d), lambda i: (i, 0)),
            pl.BlockSpec((t, LANES), lambda i: (i, 0)),
            pl.BlockSpec((None, SUBLANES, LANES), lambda i: (i, 0, 0)),
        ],
        out_shape=[
            jax.ShapeDtypeStruct((n, d), F32),
            jax.ShapeDtypeStruct((n, d), BF16),
            jax.ShapeDtypeStruct((n, LANES), F32),
            jax.ShapeDtypeStruct((nt, SUBLANES, LANES), F32),
        ],
        compiler_params=_params(("parallel",)),
        name="outproj_router",
    )(y_attn, y_rwkv, x2, w_out_bf16, gain.reshape(1, d), rw, rb)


def _run_layout(plen_f):
    plen = plen_f[:, 0, :N_EXPERTS].astype(I32)
    tot = jnp.sum(plen, axis=0)
    reg = (tot + EXPERT_BLOCK - 1) // EXPERT_BLOCK * EXPERT_BLOCK
    reg_end = jnp.cumsum(reg)
    estart = reg_end - reg
    goff = estart[None, :] + jnp.cumsum(plen, axis=0) - plen
    return plen, goff, estart + tot, reg - tot, reg_end


RUN_BITS = tuple(1 << b for b in range(ROUTE_TILE.bit_length() - 1, RUN_ALIGN.bit_length() - 2, -1))
TAIL_BITS = tuple(b for b in RUN_BITS if b < EXPERT_BLOCK)
assert RUN_BITS[0] == ROUTE_TILE and RUN_BITS[-1] == RUN_ALIGN


def _run_copies(plen_ref, goff_ref, tile, local_ref, hbm_ref, sem, to_hbm, wait):
    def body(e, lo):
        n = plen_ref[tile * N_EXPERTS + e]
        g = goff_ref[tile * N_EXPERTS + e]
        for bit in RUN_BITS:
            @pl.when((n & bit) != 0)
            def _():
                o = n & (-2 * bit)
                loc = local_ref.at[pl.ds(pl.multiple_of(lo + o, RUN_ALIGN), bit), :]
                glob = hbm_ref.at[pl.ds(pl.multiple_of(g + o, RUN_ALIGN), bit), :]
                cp = pltpu.make_async_copy(loc, glob, sem) if to_hbm else pltpu.make_async_copy(glob, loc, sem)
                if wait:
                    cp.wait()
                else:
                    cp.start()
        return lo + n
    lax.fori_loop(0, N_EXPERTS, body, 0)


def _dispatch_kernel(plen_ref, goff_ref, tstart_ref, tlen_ref, h_ref, route_ref, xs_ref,
                     buf_ref, zero_ref, sem_ref):
    t = pl.program_id(0)
    nt = pl.num_programs(0)
    slot = t & 1
    tt = h_ref.shape[0]

    pick = (lax.broadcasted_iota(I32, (SUBLANES, LANES), 1)
            == lax.broadcasted_iota(I32, (SUBLANES, LANES), 0) + 2 * TOP_K).astype(F32)
    pos_t = lax.dot_general(pick, route_ref[...], (((1,), (1,)), ((), ())),
                            precision=HIGHEST, preferred_element_type=F32).astype(I32)
    rows = lax.broadcasted_iota(I32, (TILE_ROWS, tt), 0)
    hit = rows == pos_t[0:1, :]
    for k in range(1, TOP_K):
        hit = hit | (rows == pos_t[k:k + 1, :])
    grouped = _dot(jnp.where(hit, 1.0, 0.0).astype(BF16), h_ref[...])

    for s in range(2):
        @pl.when(slot == s)
        def _():
            buf_ref[s] = grouped
            _run_copies(plen_ref, goff_ref, t, buf_ref.at[s], xs_ref, sem_ref.at[s], True, False)

    for s in range(2):
        @pl.when((slot != s) & (t > 0))
        def _():
            _run_copies(plen_ref, goff_ref, t - 1, buf_ref.at[s], xs_ref, sem_ref.at[s], True, True)

    @pl.when(t == nt - 1)
    def _():
        for s in range(2):
            @pl.when(slot == s)
            def _():
                _run_copies(plen_ref, goff_ref, t, buf_ref.at[s], xs_ref, sem_ref.at[s], True, True)
        zero_ref[...] = jnp.zeros_like(zero_ref)
        for wait in (False, True):
            def body(e, c):
                n = tlen_ref[e]
                g = tstart_ref[e]
                for bit in TAIL_BITS:
                    @pl.when((n & bit) != 0)
                    def _():
                        o = n & (-2 * bit)
                        cp = pltpu.make_async_copy(
                            zero_ref.at[pl.ds(0, bit), :],
                            xs_ref.at[pl.ds(pl.multiple_of(g + o, RUN_ALIGN), bit), :], sem_ref.at[2])
                        if wait:
                            cp.wait()
                        else:
                            cp.start()
                return c
            lax.fori_loop(0, N_EXPERTS, body, 0)

            zrows = zero_ref.shape[0]

            def slack(i, c):
                g = tstart_ref[N_EXPERTS] + i * zrows
                cp = pltpu.make_async_copy(
                    zero_ref, xs_ref.at[pl.ds(pl.multiple_of(g, RUN_ALIGN), zrows), :], sem_ref.at[2])
                if wait:
                    cp.wait()
                else:
                    cp.start()
                return c
            lax.fori_loop(0, tlen_ref[N_EXPERTS] // zrows, slack, 0)


def _dispatch(h_bf16, route, plen, goff, tstart, tlen, total_rows):
    n, d = h_bf16.shape
    t = ROUTE_TILE
    grid_spec = pltpu.PrefetchScalarGridSpec(
        num_scalar_prefetch=4,
        grid=(n // t,),
        in_specs=[
            pl.BlockSpec((t, d), lambda i, *_: (i, 0)),
            pl.BlockSpec((t, LANES), lambda i, *_: (i, 0)),
        ],
        out_specs=pl.BlockSpec(memory_space=pl.ANY),
        scratch_shapes=[
            pltpu.VMEM((2, TILE_ROWS, d), F32),
            pltpu.VMEM((EXPERT_BLOCK // 2, d), F32),
            pltpu.SemaphoreType.DMA((3,)),
        ],
    )
    return pl.pallas_call(
        _dispatch_kernel,
        grid_spec=grid_spec,
        out_shape=jax.ShapeDtypeStruct((total_rows, d), F32),
        compiler_params=_params(("arbitrary",)),
        name="moe_dispatch",
    )(plen.reshape(-1), goff.reshape(-1), tstart, tlen, h_bf16, route)


def _experts_kernel(bexp_ref, nvalid_ref, x_ref, w1_ref, b1_ref, w2_ref, b2_ref, y_ref,
                    w1b_ref, w2b_ref):
    j = pl.program_id(0)
    valid = j < nvalid_ref[0]
    changed = (j == 0) | (bexp_ref[j] != bexp_ref[jnp.maximum(j - 1, 0)])
    f = w2_ref.shape[0]

    @pl.when(valid & changed)
    def _():
        w1b_ref[...] = w1_ref[...].astype(BF16)
        w2b_ref[...] = w2_ref[...].astype(BF16)

    @pl.when(valid)
    def _():
        u = _dot(x_ref[...].astype(BF16), w1b_ref[...]) + b1_ref[...]
        glu = jnp.minimum(u[:, :f], SWIGLU_LIMIT)
        lin = jnp.clip(u[:, f:], -SWIGLU_LIMIT, SWIGLU_LIMIT)
        act = glu * _sigmoid(SWIGLU_ALPHA * glu) * (lin + 1.0)
        y_ref[...] = _dot(act.astype(BF16), w2b_ref[...]) + b2_ref[...]

    @pl.when(jnp.logical_not(valid))
    def _():
        y_ref[...] = jnp.zeros_like(y_ref)


def _experts(xs, block_expert, nvalid, w1, b1, w2, b2):
    rows, d = xs.shape
    bm = EXPERT_BLOCK
    f2 = w1.shape[2]
    f = w2.shape[1]
    grid_spec = pltpu.PrefetchScalarGridSpec(
        num_scalar_prefetch=2,
        grid=(rows // bm,),
        in_specs=[
            pl.BlockSpec((bm, d), lambda j, be, nv: (jnp.minimum(j, nv[0] - 1), 0)),
            pl.BlockSpec((None, d, f2), lambda j, be, nv: (be[j], 0, 0)),
            pl.BlockSpec((None, 1, f2), lambda j, be, nv: (be[j], 0, 0)),
            pl.BlockSpec((None, f, d), lambda j, be, nv: (be[j], 0, 0)),
            pl.BlockSpec((None, 1, d), lambda j, be, nv: (be[j], 0, 0)),
        ],
        out_specs=pl.BlockSpec((bm, d), lambda j, be, nv: (j, 0)),
        scratch_shapes=[pltpu.VMEM((d, f2), BF16), pltpu.VMEM((f, d), BF16)],
    )
    e = w1.shape[0]
    return pl.pallas_call(
        _experts_kernel,
        grid_spec=grid_spec,
        out_shape=jax.ShapeDtypeStruct((rows, d), F32),
        compiler_params=_params(("arbitrary",)),
        name="moe_experts",
    )(block_expert, nvalid, xs, w1, b1.reshape(e, 1, f2), w2, b2.reshape(e, 1, d))


def _combine_kernel(plen_ref, goff_ref, route_ref, x1_ref, g_ref, ys_ref, o_ref, buf_ref, sem_ref):
    t = pl.program_id(0)
    nt = pl.num_programs(0)
    slot = t & 1

    @pl.when(t == 0)
    def _():
        buf_ref[...] = jnp.zeros_like(buf_ref)
        _run_copies(plen_ref, goff_ref, t, buf_ref.at[0], ys_ref, sem_ref.at[0], False, False)

    for s in range(2):
        @pl.when((slot != s) & (t + 1 < nt))
        def _():
            _run_copies(plen_ref, goff_ref, t + 1, buf_ref.at[s], ys_ref, sem_ref.at[s], False, False)

    route = route_ref[...]
    tt = route.shape[0]
    col = lax.broadcasted_iota(I32, (tt, TILE_ROWS), 1)
    wts = jnp.zeros((tt, TILE_ROWS), F32)
    for k in range(TOP_K):
        pos_k = route[:, 2 * TOP_K + k:2 * TOP_K + k + 1].astype(I32)
        wts = wts + jnp.where(col == pos_k, route[:, TOP_K + k:TOP_K + k + 1], 0.0)
    wts = wts.astype(BF16)

    def finish(s):
        _run_copies(plen_ref, goff_ref, t, buf_ref.at[s], ys_ref, sem_ref.at[s], False, True)
        x2 = x1_ref[...] + _dot(wts, buf_ref[s].astype(BF16))
        ms = jnp.mean(x2 * x2, axis=-1, keepdims=True)
        o_ref[...] = x2 * lax.rsqrt(ms + NORM_EPS) * g_ref[...]

    for s in range(2):
        @pl.when(slot == s)
        def _():
            finish(s)


def _combine(ys, route, x1, gain, plen, goff):
    n, d = x1.shape
    t = ROUTE_TILE
    grid_spec = pltpu.PrefetchScalarGridSpec(
        num_scalar_prefetch=2,
        grid=(n // t,),
        in_specs=[
            pl.BlockSpec((t, LANES), lambda i, *_: (i, 0)),
            pl.BlockSpec((t, d), lambda i, *_: (i, 0)),
            pl.BlockSpec((1, d), lambda i, *_: (0, 0)),
            pl.BlockSpec(memory_space=pl.ANY),
        ],
        out_specs=pl.BlockSpec((t, d), lambda i, *_: (i, 0)),
        scratch_shapes=[
            pltpu.VMEM((2, TILE_ROWS, d), F32),
            pltpu.SemaphoreType.DMA((2,)),
        ],
    )
    return pl.pallas_call(
        _combine_kernel,
        grid_spec=grid_spec,
        out_shape=jax.ShapeDtypeStruct((n, d), F32),
        compiler_params=_params(("arbitrary",)),
        name="moe_combine",
    )(plen.reshape(-1), goff.reshape(-1), route, x1, gain.reshape(1, d), ys)


def _moe_rows(n):
    nt = n // ROUTE_TILE
    worst = TOP_K * n + nt * N_EXPERTS * (RUN_ALIGN - 1) + N_EXPERTS * (EXPERT_BLOCK - RUN_ALIGN)
    return -(-worst // EXPERT_BLOCK) * EXPERT_BLOCK


def _moe_and_final_norm(h_bf16, route, plen_f, x1, w1, b1, w2, b2, final_gain):
    n = x1.shape[0]
    rows = _moe_rows(n)
    plen, goff, tstart, tlen, reg_end = _run_layout(plen_f)
    nblocks = rows // EXPERT_BLOCK
    starts = jnp.arange(nblocks, dtype=I32) * EXPERT_BLOCK
    nvalid = (reg_end[-1] // EXPERT_BLOCK).astype(I32).reshape(1)
    bexp = jnp.searchsorted(reg_end, starts, side="right").astype(I32)
    last = jnp.take(bexp, jnp.maximum(nvalid[0] - 1, 0))
    bexp = jnp.where(starts < reg_end[-1], bexp, last)
    tstart = jnp.concatenate([tstart, reg_end[-1:]]).astype(I32)
    tlen = jnp.concatenate([tlen, rows - reg_end[-1:]]).astype(I32)
    xs = _dispatch(h_bf16, route, plen, goff, tstart, tlen, rows)
    ys = _experts(xs, bexp, nvalid, w1, b1, w2, b2)
    return _combine(ys, route, x1, final_gain, plen, goff)


def kernel(x, norm1_gain, w_in, shift_mu, decay_up, decay_bias, iclr_up, iclr_bias, gate_up, k_k, k_a, r_k, lnx_w, lnx_b, attn_out_gain, w_out, norm2_gain, router_w, router_b, expert_w1, expert_b1, expert_w2, expert_b2, final_norm_gain):
    b, s, d = x.shape
    x2 = x.reshape(b * s, d)
    z_attn, z_rwkv = _inproj(x2, norm1_gain[0], w_in[0].astype(BF16))
    y_attn = _attention(z_attn, attn_out_gain[0], b, s)
    y_rwkv = _rwkv(z_rwkv, b, s, shift_mu[0], decay_up[0], decay_bias[0], iclr_up[0], iclr_bias[0],
                   gate_up[0], k_k[0], k_a[0], r_k[0], lnx_w[0], lnx_b[0])
    x1, h2, route, plen_f = _outproj_router(y_attn, y_rwkv, x2, w_out[0].astype(BF16), norm2_gain[0],
                                            router_w[0], router_b[0])
    out = _moe_and_final_norm(h2, route, plen_f, x1, expert_w1[0], expert_b1[0], expert_w2[0],
                              expert_b2[0], final_norm_gain)
    return out.reshape(b, s, d)
```

```python
import functools

import jax
import jax.numpy as jnp
from jax import lax
from jax.experimental import pallas as pl
from jax.experimental.pallas import tpu as pltpu

F32 = jnp.float32
BF16 = jnp.bfloat16
I32 = jnp.int32

D_MODEL = 1024
HEADS = 8
HEAD_DIM = 64
ATTN_WIDTH = HEADS * HEAD_DIM
RWKV_WIDTH = HEADS * HEAD_DIM
DECAY_LORA = 64
ICLR_LORA = 64
GATE_LORA = 128
RWKV_IN = 3 * RWKV_WIDTH + DECAY_LORA + ICLR_LORA + GATE_LORA
IN_WIDTH = 3 * ATTN_WIDTH + RWKV_IN
N_EXPERTS = 32
TOP_K = 4
SWIGLU_ALPHA = 1.702
SWIGLU_LIMIT = 7.0
NORM_EPS = 1e-5
GROUPNORM_EPS = 64e-5
L2_EPS = 1e-12

LANES = 128
SUBLANES = 8
VMEM_LIMIT = 56 * 1024 * 1024
NEG_INF = float("-inf")


def _dot(a, b):
    return jnp.dot(a, b, preferred_element_type=F32)


def _dot_nt(a, b):
    return lax.dot_general(a, b, (((1,), (1,)), ((), ())), preferred_element_type=F32)


def _params(sem, vmem=VMEM_LIMIT):
    return pltpu.CompilerParams(dimension_semantics=sem, vmem_limit_bytes=vmem)


def _inproj_kernel(x_ref, g_ref, w_ref, za_ref, zr_ref):
    x = x_ref[...]
    ms = jnp.mean(x * x, axis=-1, keepdims=True)
    h = (x * lax.rsqrt(ms + NORM_EPS) * g_ref[...]).astype(BF16)
    na = za_ref.shape[1]
    za_ref[...] = _dot(h, w_ref[:, :na])
    zr_ref[...] = _dot(h, w_ref[:, na:])


def _inproj(x2, gain, w_in_bf16, tm=512):
    n, d = x2.shape
    na = 3 * ATTN_WIDTH
    return pl.pallas_call(
        _inproj_kernel,
        grid=(n // tm,),
        in_specs=[
            pl.BlockSpec((tm, d), lambda i: (i, 0)),
            pl.BlockSpec((1, d), lambda i: (0, 0)),
            pl.BlockSpec((d, IN_WIDTH), lambda i: (0, 0)),
        ],
        out_specs=[
            pl.BlockSpec((tm, na), lambda i: (i, 0)),
            pl.BlockSpec((tm, RWKV_IN), lambda i: (i, 0)),
        ],
        out_shape=[
            jax.ShapeDtypeStruct((n, na), F32),
            jax.ShapeDtypeStruct((n, RWKV_IN), F32),
        ],
        compiler_params=_params(("parallel",)),
        name="inproj",
    )(x2, gain.reshape(1, d), w_in_bf16)


ATT_BLK = 128
ATT_DIL = 16
ATT_BLOCKS_PER_ITER = 4
ATT_CLASSES_PER_ITER = 4
LOG2 = 0.6931471805599453


def _attn_kernel(q_ref, k_ref, v_ref, g_ref, o_ref,
                 kp_ref, vp_ref, m1_ref, l1_ref, a1_ref, of_ref):
    seq = q_ref.shape[0]
    nblk = seq // ATT_BLK
    scale = HEAD_DIM ** -0.5
    lane = lax.broadcasted_iota(I32, (1, LANES), 1)
    head_masks = (lane < HEAD_DIM, lane >= HEAD_DIM)

    for r in range(ATT_DIL):
        pb = ((r % 4) * 4 + r // 4) * ATT_BLK
        kp_ref[pb:pb + ATT_BLK, :] = k_ref[pl.ds(r, ATT_BLK, stride=ATT_DIL), :].astype(BF16)
        vp_ref[pb:pb + ATT_BLK, :] = v_ref[pl.ds(r, ATT_BLK, stride=ATT_DIL), :].astype(BF16)

    ii = lax.broadcasted_iota(I32, (ATT_BLK, ATT_BLK), 0)
    jj = lax.broadcasted_iota(I32, (ATT_BLK, ATT_BLK), 1)

    def pack_heads(vals):
        return jnp.where(head_masks[0], vals[0], vals[1])

    cur_ok = jj <= ii
    prev_band = jj >= ii

    def blk_group(g, carry):
        base = g * ATT_BLOCKS_PER_ITER
        rows = [pl.ds(pl.multiple_of((base + u) * ATT_BLK, ATT_BLK), ATT_BLK)
                for u in range(ATT_BLOCKS_PER_ITER)]
        p0 = pl.ds(pl.multiple_of(jnp.maximum(base - 1, 0) * ATT_BLK, ATT_BLK), ATT_BLK)
        qs = [q_ref[rw, :] * scale for rw in rows]
        ks = [k_ref[p0, :].astype(BF16)] + [k_ref[rw, :].astype(BF16) for rw in rows]
        vs = [v_ref[p0, :].astype(BF16)] + [v_ref[rw, :].astype(BF16) for rw in rows]
        prev_ok = [prev_band & (g > 0)] + [prev_band] * (ATT_BLOCKS_PER_ITER - 1)
        chains = [(u, h) for u in range(ATT_BLOCKS_PER_ITER) for h in range(2)]
        qh = [jnp.where(head_masks[h], qs[u], 0.0).astype(BF16) for u, h in chains]
        sc = [jnp.where(cur_ok, _dot_nt(q, ks[u + 1]), NEG_INF) for q, (u, h) in zip(qh, chains)]
        sp = [jnp.where(prev_ok[u], _dot_nt(q, ks[u]), NEG_INF) for q, (u, h) in zip(qh, chains)]
        m = [jnp.max(jnp.maximum(a, b), axis=-1, keepdims=True) for a, b in zip(sc, sp)]
        pc = [jnp.exp(a - mm) for a, mm in zip(sc, m)]
        pp = [jnp.exp(b - mm) for b, mm in zip(sp, m)]
        ls = [jnp.sum(a + b, axis=-1, keepdims=True) for a, b in zip(pc, pp)]
        acc = [_dot(a.astype(BF16), vs[u + 1]) + _dot(b.astype(BF16), vs[u])
               for a, b, (u, h) in zip(pc, pp, chains)]
        for u, rw in enumerate(rows):
            m1_ref[rw, :] = pack_heads(m[2 * u:2 * u + 2])
            l1_ref[rw, :] = pack_heads(ls[2 * u:2 * u + 2])
            a1_ref[rw, :] = pack_heads(acc[2 * u:2 * u + 2])
        return carry

    lax.fori_loop(0, nblk // ATT_BLOCKS_PER_ITER, blk_group, 0)

    nk = 4 * ATT_BLK
    qi = lax.broadcasted_iota(I32, (ATT_BLK, nk), 0)
    kc_i = lax.broadcasted_iota(I32, (ATT_BLK, nk), 1)
    k_hi = kc_i >> 7
    k_pos = kc_i & (ATT_BLK - 1)
    gain = g_ref[...]

    def res_group(g, carry):
        per_hi = 4 // ATT_CLASSES_PER_ITER
        r_hi = g // per_hi
        rs = [4 * r_hi + (g % per_hi) * ATT_CLASSES_PER_ITER + u for u in range(ATT_CLASSES_PER_ITER)]
        rows = [pl.ds(r, ATT_BLK, stride=ATT_DIL) for r in rs]
        qr = [q_ref[rw, :] * scale for rw in rows]
        k0 = [pl.ds(pl.multiple_of((r & 3) * nk, nk), nk) for r in rs]
        kc = [kp_ref[kk, :] for kk in k0]
        vc = [vp_ref[kk, :] for kk in k0]
        delta = ATT_DIL * (qi - k_pos) + 4 * (r_hi - k_hi)
        in4 = (delta >= 0) & (delta <= 4 * ATT_BLK)
        in16 = (k_hi == r_hi) & (delta >= 0)
        bias = jnp.where(in4 & in16, LOG2, jnp.where(in4 | in16, 0.0, NEG_INF))
        chains = [(u, h) for u in range(ATT_CLASSES_PER_ITER) for h in range(2)]
        qh = [jnp.where(head_masks[h], qr[u], 0.0).astype(BF16) for u, h in chains]
        s = [_dot_nt(q, kc[u]) + bias for q, (u, h) in zip(qh, chains)]
        m = [jnp.max(a, axis=-1, keepdims=True) for a in s]
        e = [jnp.exp(a - mm) for a, mm in zip(s, m)]
        ls = [jnp.sum(a, axis=-1, keepdims=True) for a in e]
        acc = [_dot(a.astype(BF16), vc[u]) for a, (u, h) in zip(e, chains)]
        for u, rw in enumerate(rows):
            m2, l2, a2 = (pack_heads(x[2 * u:2 * u + 2]) for x in (m, ls, acc))
            m1 = m1_ref[rw, :]
            mx = jnp.maximum(m1, m2)
            w1 = jnp.exp(m1 - mx)
            w2 = jnp.exp(m2 - mx)
            den = l1_ref[rw, :] * w1 + l2 * w2
            o = (a1_ref[rw, :] * w1 + a2 * w2) / den
            o2 = o * o
            msq = pack_heads([jnp.sum(jnp.where(hm, o2, 0.0), axis=-1, keepdims=True)
                              for hm in head_masks])
            of_ref[rw, :] = o * lax.rsqrt(msq * (1.0 / HEAD_DIM) + NORM_EPS) * gain
        return carry

    lax.fori_loop(0, ATT_DIL // ATT_CLASSES_PER_ITER, res_group, 0)
    o_ref[...] = of_ref[...].astype(o_ref.dtype)


def _attention(z_attn, out_gain, batch, seq):
    assert seq == ATT_BLK * ATT_DIL
    pairs = ATTN_WIDTH // LANES
    z3 = z_attn.reshape(batch, seq, 3 * ATTN_WIDTH)
    blk = lambda off: pl.BlockSpec((None, seq, LANES), lambda b, p: (b, 0, off + p))
    out = pl.pallas_call(
        _attn_kernel,
        grid=(batch, pairs),
        in_specs=[blk(0), blk(pairs), blk(2 * pairs),
                  pl.BlockSpec((1, LANES), lambda b, p: (0, p))],
        out_specs=pl.BlockSpec((None, seq, LANES), lambda b, p: (b, 0, p)),
        out_shape=jax.ShapeDtypeStruct((batch, seq, ATTN_WIDTH), BF16),
        scratch_shapes=[
            pltpu.VMEM((seq, LANES), BF16), pltpu.VMEM((seq, LANES), BF16),
            pltpu.VMEM((seq, LANES), F32), pltpu.VMEM((seq, LANES), F32),
            pltpu.VMEM((seq, LANES), F32), pltpu.VMEM((seq, LANES), F32),
        ],
        compiler_params=_params(("parallel", "parallel")),
        name="dilated_attn",
    )(z3, z3, z3, out_gain.reshape(1, ATTN_WIDTH))
    return out.reshape(batch * seq, ATTN_WIDTH)


WKV_CHUNK = 64
WKV_SEQS_PER_STEP = 4


def _split3(x):
    h = x.astype(BF16)
    r1 = x - h.astype(F32)
    m = r1.astype(BF16)
    l = (r1 - m.astype(F32)).astype(BF16)
    return h, m, l


def _seg_sum(x, seg):
    hi = x.astype(BF16)
    lo = (x - hi.astype(F32)).astype(BF16)
    outs = []
    for p in range(x.shape[1] // LANES):
        sl = slice(p * LANES, (p + 1) * LANES)
        outs.append(_dot(hi[:, sl], seg) + _dot(lo[:, sl], seg))
    return jnp.concatenate(outs, axis=1)


def _softplus(y):
    return jnp.maximum(y, 0.0) + jnp.log(1.0 + jnp.exp(-jnp.abs(y)))


def _sigmoid(y):
    return 1.0 / (1.0 + jnp.exp(-y))


def _rwkv_kernel(z_ref, mu_ref, wl_ref, db_ref, ib_ref, gu_ref, kk_ref, ka_ref, rk_ref,
                 lnw_ref, lnb_ref, o_ref, st_ref, prev_ref):
    nseq, cs, _ = z_ref.shape
    w = RWKV_WIDTH
    pairs = w // LANES
    n2 = 2 * cs
    assert n2 == LANES

    @pl.when(pl.program_id(1) == 0)
    def _():
        st_ref[...] = jnp.zeros_like(st_ref)
        prev_ref[...] = jnp.zeros_like(prev_ref)

    lane = lax.broadcasted_iota(I32, (1, LANES), 1)
    head_a = lane < HEAD_DIM
    li = lax.broadcasted_iota(I32, (LANES, LANES), 0)
    lj = lax.broadcasted_iota(I32, (LANES, LANES), 1)
    seg = ((li >> 6) == (lj >> 6)).astype(BF16)
    ti = lax.broadcasted_iota(I32, (cs, cs), 0)
    tj = lax.broadcasted_iota(I32, (cs, cs), 1)
    tri = (ti >= tj).astype(BF16)
    strict = li < lj
    incl = li <= lj
    eye = (li == lj).astype(F32)
    eye_b = eye.astype(BF16)
    sel = ((lax.broadcasted_iota(I32, (cs, LANES), 1) & (cs - 1))
           == lax.broadcasted_iota(I32, (cs, LANES), 0)).astype(BF16)
    row = lax.broadcasted_iota(I32, (cs, 1), 0)

    def stack2(x):
        return jnp.concatenate([jnp.where(head_a, x, 0.0), jnp.where(head_a, 0.0, x)], axis=0).astype(BF16)

    seqs = []
    for i in range(nseq):
        z = z_ref[i]
        zprev = jnp.where(row == 0, prev_ref[i, 0:1, :], pltpu.roll(z, 1, axis=0))
        prev_ref[i, 0:1, :] = z[cs - 1:cs, :]
        zs = z + (zprev - z) * mu_ref[...]
        r_ = zs[:, 0:w]
        k_ = zs[:, w:2 * w]
        v_ = zs[:, 2 * w:3 * w]
        lora_in = zs[:, 3 * w:3 * w + LANES]
        gate_in = zs[:, 3 * w + LANES:]
        xl = jnp.where(head_a, jnp.tanh(lora_in), lora_in).astype(BF16)
        pre = _dot(xl, wl_ref[...])
        wlog = -_softplus(-(pre[:, :w] + db_ref[...])) - 0.5
        lw = -jnp.exp(wlog)
        rate = _sigmoid(pre[:, w:] + ib_ref[...])
        gate = _dot(_sigmoid(gate_in).astype(BF16), gu_ref[...])
        kk = k_ * kk_ref[...]
        kk = kk / jnp.maximum(jnp.sqrt(_seg_sum(kk * kk, seg)), L2_EPS)
        kp = k_ * (1.0 + (rate - 1.0) * ka_ref[...])
        bn = kk * rate
        cum = sum(_dot(tri, t) for t in _split3(lw))
        cend = cum[cs - 1:cs, :]
        e_inv = jnp.exp(-cum)
        e_end = jnp.exp(cend - cum)
        seqs.append(dict(
            r=r_, v=v_, kp=kp, gate=gate, g_end=jnp.exp(cend),
            at=-kk * jnp.exp(cum - lw), rt=r_ * jnp.exp(cum), bt=bn * e_inv, kt=kp * e_inv,
            bp=bn * e_end, kpe=kp * e_end))

    chains = [(i, p) for i in range(nseq) for p in range(pairs)]
    sl = lambda p: slice(p * LANES, (p + 1) * LANES)
    ar = [jnp.concatenate([stack2(seqs[i]["at"][:, sl(p)]), stack2(seqs[i]["rt"][:, sl(p)])], axis=0)
          for i, p in chains]
    bk = [jnp.concatenate([stack2(seqs[i]["bt"][:, sl(p)]), stack2(seqs[i]["kt"][:, sl(p)])], axis=0)
          for i, p in chains]
    g1 = [_dot_nt(b, a) for b, a in zip(bk, ar)]
    ab_t = [jnp.where(strict, g[:n2, :n2], 0.0) for g in g1]
    rb_t = [jnp.where(incl, g[:n2, n2:], 0.0).astype(BF16) for g in g1]
    akrk = [jnp.concatenate([jnp.where(strict, g[n2:, :n2], 0.0), jnp.where(incl, g[n2:, n2:], 0.0)],
                            axis=1).astype(BF16) for g in g1]

    tt = [eye + jnp.where((li >> 1) == (lj >> 1), a, 0.0) for a in ab_t]
    s = 2
    while s < cs:
        sh = s.bit_length() - 1
        off = ((li >> (sh + 1)) == (lj >> (sh + 1))) & ((li >> sh) != (lj >> sh))
        tb = [t.astype(BF16) for t in tt]
        ta = [_dot(t, jnp.where(off, a, 0.0).astype(BF16)).astype(BF16) for t, a in zip(tb, ab_t)]
        tt = [t + _dot(x, b) for t, x, b in zip(tt, ta, tb)]
        s *= 2
    tt = [t.astype(BF16) for t in tt]

    v_t = [_dot_nt(eye_b, stack2(seqs[i]["v"][:, sl(p)])).astype(BF16) for i, p in chains]
    s0 = [st_ref[i, p] for i, p in chains]
    g2 = [_dot_nt(s.astype(BF16), a) for s, a in zip(s0, ar)]
    g3 = [_dot(v, m) for v, m in zip(v_t, akrk)]
    u_t = [_dot((a[:, :n2] + b[:, :n2]).astype(BF16), t).astype(BF16) for a, b, t in zip(g2, g3, tt)]
    y_t = [a[:, n2:] + b[:, n2:] + _dot(u, m) for a, b, u, m in zip(g2, g3, u_t, rb_t)]
    for c, (i, p) in enumerate(chains):
        bkp = jnp.concatenate([stack2(seqs[i]["bp"][:, sl(p)]), stack2(seqs[i]["kpe"][:, sl(p)])], axis=0)
        st_ref[i, p] = s0[c] * seqs[i]["g_end"][:, sl(p)] + _dot(
            jnp.concatenate([u_t[c], v_t[c]], axis=1), bkp)
    ys = [_dot_nt(sel, y.astype(BF16)) for y in y_t]

    inv_n = 1.0 / HEAD_DIM
    for i in range(nseq):
        q = seqs[i]
        y = jnp.concatenate(ys[i * pairs:(i + 1) * pairs], axis=1)
        mean = _seg_sum(y, seg) * inv_n
        yc = y - mean
        var = _seg_sum(yc * yc, seg) * inv_n
        y = yc * lax.rsqrt(var + GROUPNORM_EPS) * lnw_ref[...] + lnb_ref[...]
        y = y + _seg_sum(q["r"] * q["kp"] * rk_ref[...], seg) * q["v"]
        o_ref[i] = (y * q["gate"]).astype(o_ref.dtype)


def _rwkv(z_rwkv, batch, seq, shift_mu, decay_up, decay_bias, iclr_up, iclr_bias, gate_up,
          k_k, k_a, r_k, lnx_w, lnx_b):
    cs = WKV_CHUNK
    w = RWKV_WIDTH
    nb = WKV_SEQS_PER_STEP if batch % WKV_SEQS_PER_STEP == 0 else 1
    z3 = z_rwkv.reshape(batch, seq, RWKV_IN)
    wl = jnp.zeros((LANES, 2 * w), F32)
    wl = wl.at[:DECAY_LORA, :w].set(decay_up).at[DECAY_LORA:, w:].set(iclr_up).astype(BF16)
    row = lambda a: a.reshape(1, -1).astype(F32)
    full = lambda shape: pl.BlockSpec(shape, lambda b, c: (0,) * len(shape))
    out = pl.pallas_call(
        _rwkv_kernel,
        grid=(batch // nb, seq // cs),
        in_specs=[
            pl.BlockSpec((nb, cs, RWKV_IN), lambda b, c: (b, c, 0)),
            full((1, RWKV_IN)), full((LANES, 2 * w)), full((1, w)), full((1, w)),
            full((GATE_LORA, w)), full((1, w)), full((1, w)), full((1, w)), full((1, w)), full((1, w)),
        ],
        out_specs=pl.BlockSpec((nb, cs, w), lambda b, c: (b, c, 0)),
        out_shape=jax.ShapeDtypeStruct((batch, seq, w), BF16),
        scratch_shapes=[
            pltpu.VMEM((nb, w // LANES, LANES, LANES), F32),
            pltpu.VMEM((nb, SUBLANES, RWKV_IN), F32),
        ],
        compiler_params=_params(("parallel", "arbitrary")),
        name="rwkv7",
    )(z3, row(shift_mu), wl, row(decay_bias), row(iclr_bias), gate_up.astype(BF16),
      row(k_k), row(k_a), row(r_k), row(lnx_w), row(lnx_b))
    return out.reshape(batch * seq, w)


ROUTE_TILE = 256
RUN_ALIGN = SUBLANES
TILE_ROWS = 1280
EXPERT_BLOCK = 512
assert TILE_ROWS >= TOP_K * ROUTE_TILE + N_EXPERTS * (RUN_ALIGN - 1) and TILE_ROWS % LANES == 0
HIGHEST = lax.Precision.HIGHEST


def _router_kernel(ya_ref, yr_ref, x_ref, wo_ref, g_ref, rw_ref, rb_ref,
                   x1_ref, h_ref, route_ref, plen_ref):
    t = x_ref.shape[0]
    half = ya_ref.shape[1]
    x1 = x_ref[...] + _dot(ya_ref[...], wo_ref[:half, :]) + _dot(yr_ref[...], wo_ref[half:, :])
    x1_ref[...] = x1
    ms = jnp.mean(x1 * x1, axis=-1, keepdims=True)
    h = x1 * lax.rsqrt(ms + NORM_EPS) * g_ref[...]
    h_ref[...] = h.astype(BF16)

    lane = lax.broadcasted_iota(I32, (t, LANES), 1)
    lane_f = lane.astype(F32)
    logits = jnp.dot(h, rw_ref[...], precision=HIGHEST, preferred_element_type=F32) + rb_ref[...]
    lg = jnp.where(lane < N_EXPERTS, logits, NEG_INF)
    vals, hots = [], []
    for _ in range(TOP_K):
        m = jnp.max(lg, axis=-1, keepdims=True)
        idx = jnp.min(jnp.where(lg == m, lane_f, float(LANES)), axis=-1, keepdims=True)
        hot = lane_f == idx
        vals.append(m)
        hots.append(hot)
        lg = jnp.where(hot, NEG_INF, lg)
    exps = [jnp.exp(v - vals[0]) for v in vals]
    den = exps[0] + exps[1] + exps[2] + exps[3]

    multi = sum(hh.astype(F32) for hh in hots)
    ti = lax.broadcasted_iota(I32, (t, t), 0)
    tj = lax.broadcasted_iota(I32, (t, t), 1)
    before = _dot((tj < ti).astype(BF16), multi.astype(BF16))
    counts = jnp.sum(multi, axis=0, keepdims=True)
    padded = jnp.ceil(counts * (1.0 / RUN_ALIGN)) * RUN_ALIGN
    li = lax.broadcasted_iota(I32, (LANES, LANES), 0)
    lj = lax.broadcasted_iota(I32, (LANES, LANES), 1)
    run_start = jnp.dot(jnp.broadcast_to(padded, (SUBLANES, LANES)), (li < lj).astype(F32),
                        precision=HIGHEST, preferred_element_type=F32)[0:1, :]
    pos = run_start + before

    route = jnp.zeros((t, LANES), F32)
    for k in range(TOP_K):
        idx_f = jnp.sum(jnp.where(hots[k], lane_f, 0.0), axis=-1, keepdims=True)
        pos_k = jnp.sum(jnp.where(hots[k], pos, 0.0), axis=-1, keepdims=True)
        route = route + jnp.where(lane == k, idx_f, 0.0)
        route = route + jnp.where(lane == TOP_K + k, exps[k] / den, 0.0)
        route = route + jnp.where(lane == 2 * TOP_K + k, pos_k, 0.0)
    route_ref[...] = route
    plen_ref[...] = jnp.broadcast_to(padded, (SUBLANES, LANES))


def _outproj_router(y_attn, y_rwkv, x2, w_out_bf16, gain, router_w, router_b):
    n, d = x2.shape
    t = ROUTE_TILE
    nt = n // t
    half = y_attn.shape[1]
    rw = jnp.zeros((d, LANES), F32).at[:, :N_EXPERTS].set(router_w)
    rb = jnp.zeros((1, LANES), F32).at[0, :N_EXPERTS].set(router_b)
    full = lambda shape: pl.BlockSpec(shape, lambda i: (0,) * len(shape))
    return pl.pallas_call(
        _router_kernel,
        grid=(nt,),
        in_specs=[
            pl.BlockSpec((t, half), lambda i: (i, 0)),
            pl.BlockSpec((t, half), lambda i: (i, 0)),
            pl.BlockSpec((t, d), lambda i: (i, 0)),
            full((2 * half, d)), full((1, d)), full((d, LANES)), full((1, LANES)),
        ],
        out_specs=[
            pl.BlockSpec((t, d), lambda i: (i, 0)),
            pl.BlockSpec((t, d), lambda i: (i, 0)),
            pl.BlockSpec((t, LANES), lambda i: (i, 0)),
            pl.BlockSpec((None, SUBLANES, LANES), lambda i: (i, 0, 0)),
        ],
        out_shape=[
            jax.ShapeDtypeStruct((n, d), F32),
            jax.ShapeDtypeStruct((n, d), BF16),
            jax.ShapeDtypeStruct((n, LANES), F32),
            jax.ShapeDtypeStruct((nt, SUBLANES, LANES), F32),
        ],
        compiler_params=_params(("parallel",)),
        name="outproj_router",
    )(y_attn, y_rwkv, x2, w_out_bf16, gain.reshape(1, d), rw, rb)


def _run_layout(plen_f):
    plen = plen_f[:, 0, :N_EXPERTS].astype(I32)
    tot = jnp.sum(plen, axis=0)
    reg = (tot + EXPERT_BLOCK - 1) // EXPERT_BLOCK * EXPERT_BLOCK
    reg_end = jnp.cumsum(reg)
    estart = reg_end - reg
    goff = estart[None, :] + jnp.cumsum(plen, axis=0) - plen
    return plen, goff, estart + tot, reg - tot, reg_end


RUN_BITS = tuple(1 << b for b in range(ROUTE_TILE.bit_length() - 1, RUN_ALIGN.bit_length() - 2, -1))
TAIL_BITS = tuple(b for b in RUN_BITS if b < EXPERT_BLOCK)
assert RUN_BITS[0] == ROUTE_TILE and RUN_BITS[-1] == RUN_ALIGN


def _run_copies(plen_ref, goff_ref, tile, local_ref, hbm_ref, sem, to_hbm, wait):
    def body(e, lo):
        n = plen_ref[tile * N_EXPERTS + e]
        g = goff_ref[tile * N_EXPERTS + e]
        for bit in RUN_BITS:
            @pl.when((n & bit) != 0)
            def _():
                o = n & (-2 * bit)
                loc = local_ref.at[pl.ds(pl.multiple_of(lo + o, RUN_ALIGN), bit), :]
                glob = hbm_ref.at[pl.ds(pl.multiple_of(g + o, RUN_ALIGN), bit), :]
                cp = pltpu.make_async_copy(loc, glob, sem) if to_hbm else pltpu.make_async_copy(glob, loc, sem)
                if wait:
                    cp.wait()
                else:
                    cp.start()
        return lo + n
    lax.fori_loop(0, N_EXPERTS, body, 0)


def _dispatch_kernel(plen_ref, goff_ref, tstart_ref, tlen_ref, h_ref, route_ref, xs_ref,
                     buf_ref, zero_ref, sem_ref):
    t = pl.program_id(0)
    nt = pl.num_programs(0)
    slot = t & 1
    tt = h_ref.shape[0]

    pick = (lax.broadcasted_iota(I32, (SUBLANES, LANES), 1)
            == lax.broadcasted_iota(I32, (SUBLANES, LANES), 0) + 2 * TOP_K).astype(F32)
    pos_t = lax.dot_general(pick, route_ref[...], (((1,), (1,)), ((), ())),
                            precision=HIGHEST, preferred_element_type=F32).astype(I32)
    rows = lax.broadcasted_iota(I32, (TILE_ROWS, tt), 0)
    hit = rows == pos_t[0:1, :]
    for k in range(1, TOP_K):
        hit = hit | (rows == pos_t[k:k + 1, :])
    grouped = _dot(jnp.where(hit, 1.0, 0.0).astype(BF16), h_ref[...])

    for s in range(2):
        @pl.when(slot == s)
        def _():
            buf_ref[s] = grouped
            _run_copies(plen_ref, goff_ref, t, buf_ref.at[s], xs_ref, sem_ref.at[s], True, False)

    for s in range(2):
        @pl.when((slot != s) & (t > 0))
        def _():
            _run_copies(plen_ref, goff_ref, t - 1, buf_ref.at[s], xs_ref, sem_ref.at[s], True, True)

    @pl.when(t == nt - 1)
    def _():
        for s in range(2):
            @pl.when(slot == s)
            def _():
                _run_copies(plen_ref, goff_ref, t, buf_ref.at[s], xs_ref, sem_ref.at[s], True, True)
        zero_ref[...] = jnp.zeros_like(zero_ref)
        for wait in (False, True):
            def body(e, c):
                n = tlen_ref[e]
                g = tstart_ref[e]
                for bit in TAIL_BITS:
                    @pl.when((n & bit) != 0)
                    def _():
                        o = n & (-2 * bit)
                        cp = pltpu.make_async_copy(
                            zero_ref.at[pl.ds(0, bit), :],
                            xs_ref.at[pl.ds(pl.multiple_of(g + o, RUN_ALIGN), bit), :], sem_ref.at[2])
                        if wait:
                            cp.wait()
                        else:
                            cp.start()
                return c
            lax.fori_loop(0, N_EXPERTS, body, 0)

            zrows = zero_ref.shape[0]

            def slack(i, c):
                g = tstart_ref[N_EXPERTS] + i * zrows
                cp = pltpu.make_async_copy(
                    zero_ref, xs_ref.at[pl.ds(pl.multiple_of(g, RUN_ALIGN), zrows), :], sem_ref.at[2])
                if wait:
                    cp.wait()
                else:
                    cp.start()
                return c
            lax.fori_loop(0, tlen_ref[N_EXPERTS] // zrows, slack, 0)


def _dispatch(h_bf16, route, plen, goff, tstart, tlen, total_rows):
    n, d = h_bf16.shape
    t = ROUTE_TILE
    grid_spec = pltpu.PrefetchScalarGridSpec(
        num_scalar_prefetch=4,
        grid=(n // t,),
        in_specs=[
            pl.BlockSpec((t, d), lambda i, *_: (i, 0)),
            pl.BlockSpec((t, LANES), lambda i, *_: (i, 0)),
        ],
        out_specs=pl.BlockSpec(memory_space=pl.ANY),
        scratch_shapes=[
            pltpu.VMEM((2, TILE_ROWS, d), F32),
            pltpu.VMEM((EXPERT_BLOCK // 2, d), F32),
            pltpu.SemaphoreType.DMA((3,)),
        ],
    )
    return pl.pallas_call(
        _dispatch_kernel,
        grid_spec=grid_spec,
        out_shape=jax.ShapeDtypeStruct((total_rows, d), F32),
        compiler_params=_params(("arbitrary",)),
        name="moe_dispatch",
    )(plen.reshape(-1), goff.reshape(-1), tstart, tlen, h_bf16, route)


def _experts_kernel(bexp_ref, nvalid_ref, x_ref, w1_ref, b1_ref, w2_ref, b2_ref, y_ref,
                    w1b_ref, w2b_ref):
    j = pl.program_id(0)
    valid = j < nvalid_ref[0]
    changed = (j == 0) | (bexp_ref[j] != bexp_ref[jnp.maximum(j - 1, 0)])
    f = w2_ref.shape[0]

    @pl.when(valid & changed)
    def _():
        w1b_ref[...] = w1_ref[...].astype(BF16)
        w2b_ref[...] = w2_ref[...].astype(BF16)

    @pl.when(valid)
    def _():
        u = _dot(x_ref[...].astype(BF16), w1b_ref[...]) + b1_ref[...]
        glu = jnp.minimum(u[:, :f], SWIGLU_LIMIT)
        lin = jnp.clip(u[:, f:], -SWIGLU_LIMIT, SWIGLU_LIMIT)
        act = glu * _sigmoid(SWIGLU_ALPHA * glu) * (lin + 1.0)
        y_ref[...] = _dot(act.astype(BF16), w2b_ref[...]) + b2_ref[...]

    @pl.when(jnp.logical_not(valid))
    def _():
        y_ref[...] = jnp.zeros_like(y_ref)


def _experts(xs, block_expert, nvalid, w1, b1, w2, b2):
    rows, d = xs.shape
    bm = EXPERT_BLOCK
    f2 = w1.shape[2]
    f = w2.shape[1]
    grid_spec = pltpu.PrefetchScalarGridSpec(
        num_scalar_prefetch=2,
        grid=(rows // bm,),
        in_specs=[
            pl.BlockSpec((bm, d), lambda j, be, nv: (jnp.minimum(j, nv[0] - 1), 0)),
            pl.BlockSpec((None, d, f2), lambda j, be, nv: (be[j], 0, 0)),
            pl.BlockSpec((None, 1, f2), lambda j, be, nv: (be[j], 0, 0)),
            pl.BlockSpec((None, f, d), lambda j, be, nv: (be[j], 0, 0)),
            pl.BlockSpec((None, 1, d), lambda j, be, nv: (be[j], 0, 0)),
        ],
        out_specs=pl.BlockSpec((bm, d), lambda j, be, nv: (j, 0)),
        scratch_shapes=[pltpu.VMEM((d, f2), BF16), pltpu.VMEM((f, d), BF16)],
    )
    e = w1.shape[0]
    return pl.pallas_call(
        _experts_kernel,
        grid_spec=grid_spec,
        out_shape=jax.ShapeDtypeStruct((rows, d), F32),
        compiler_params=_params(("arbitrary",)),
        name="moe_experts",
    )(block_expert, nvalid, xs, w1, b1.reshape(e, 1, f2), w2, b2.reshape(e, 1, d))


def _combine_kernel(plen_ref, goff_ref, route_ref, x1_ref, g_ref, ys_ref, o_ref, buf_ref, sem_ref):
    t = pl.program_id(0)
    nt = pl.num_programs(0)
    slot = t & 1

    @pl.when(t == 0)
    def _():
        buf_ref[...] = jnp.zeros_like(buf_ref)
        _run_copies(plen_ref, goff_ref, t, buf_ref.at[0], ys_ref, sem_ref.at[0], False, False)

    for s in range(2):
        @pl.when((slot != s) & (t + 1 < nt))
        def _():
            _run_copies(plen_ref, goff_ref, t + 1, buf_ref.at[s], ys_ref, sem_ref.at[s], False, False)

    route = route_ref[...]
    tt = route.shape[0]
    col = lax.broadcasted_iota(I32, (tt, TILE_ROWS), 1)
    wts = jnp.zeros((tt, TILE_ROWS), F32)
    for k in range(TOP_K):
        pos_k = route[:, 2 * TOP_K + k:2 * TOP_K + k + 1].astype(I32)
        wts = wts + jnp.where(col == pos_k, route[:, TOP_K + k:TOP_K + k + 1], 0.0)
    wts = wts.astype(BF16)

    def finish(s):
        _run_copies(plen_ref, goff_ref, t, buf_ref.at[s], ys_ref, sem_ref.at[s], False, True)
        x2 = x1_ref[...] + _dot(wts, buf_ref[s].astype(BF16))
        ms = jnp.mean(x2 * x2, axis=-1, keepdims=True)
        o_ref[...] = x2 * lax.rsqrt(ms + NORM_EPS) * g_ref[...]

    for s in range(2):
        @pl.when(slot == s)
        def _():
            finish(s)


def _combine(ys, route, x1, gain, plen, goff):
    n, d = x1.shape
    t = ROUTE_TILE
    grid_spec = pltpu.PrefetchScalarGridSpec(
        num_scalar_prefetch=2,
        grid=(n // t,),
        in_specs=[
            pl.BlockSpec((t, LANES), lambda i, *_: (i, 0)),
            pl.BlockSpec((t, d), lambda i, *_: (i, 0)),
            pl.BlockSpec((1, d), lambda i, *_: (0, 0)),
            pl.BlockSpec(memory_space=pl.ANY),
        ],
        out_specs=pl.BlockSpec((t, d), lambda i, *_: (i, 0)),
        scratch_shapes=[
            pltpu.VMEM((2, TILE_ROWS, d), F32),
            pltpu.SemaphoreType.DMA((2,)),
        ],
    )
    return pl.pallas_call(
        _combine_kernel,
        grid_spec=grid_spec,
        out_shape=jax.ShapeDtypeStruct((n, d), F32),
        compiler_params=_params(("arbitrary",)),
        name="moe_combine",
    )(plen.reshape(-1), goff.reshape(-1), route, x1, gain.reshape(1, d), ys)


def _moe_rows(n):
    nt = n // ROUTE_TILE
    worst = TOP_K * n + nt * N_EXPERTS * (RUN_ALIGN - 1) + N_EXPERTS * (EXPERT_BLOCK - RUN_ALIGN)
    return -(-worst // EXPERT_BLOCK) * EXPERT_BLOCK


def _moe_and_final_norm(h_bf16, route, plen_f, x1, w1, b1, w2, b2, final_gain):
    n = x1.shape[0]
    rows = _moe_rows(n)
    plen, goff, tstart, tlen, reg_end = _run_layout(plen_f)
    nblocks = rows // EXPERT_BLOCK
    starts = jnp.arange(nblocks, dtype=I32) * EXPERT_BLOCK
    nvalid = (reg_end[-1] // EXPERT_BLOCK).astype(I32).reshape(1)
    bexp = jnp.sum((reg_end[None, :] <= starts[:, None]).astype(I32), axis=1)
    last = jnp.take(bexp, jnp.maximum(nvalid[0] - 1, 0))
    bexp = jnp.where(starts < reg_end[-1], bexp, last)
    tstart = jnp.concatenate([tstart, reg_end[-1:]]).astype(I32)
    tlen = jnp.concatenate([tlen, rows - reg_end[-1:]]).astype(I32)
    xs = _dispatch(h_bf16, route, plen, goff, tstart, tlen, rows)
    ys = _experts(xs, bexp, nvalid, w1, b1, w2, b2)
    return _combine(ys, route, x1, final_gain, plen, goff)


def kernel(x, norm1_gain, w_in, shift_mu, decay_up, decay_bias, iclr_up, iclr_bias, gate_up, k_k, k_a, r_k, lnx_w, lnx_b, attn_out_gain, w_out, norm2_gain, router_w, router_b, expert_w1, expert_b1, expert_w2, expert_b2, final_norm_gain):
    b, s, d = x.shape
    x2 = x.reshape(b * s, d)
    z_attn, z_rwkv = _inproj(x2, norm1_gain[0], w_in[0].astype(BF16))
    y_attn = _attention(z_attn, attn_out_gain[0], b, s)
    y_rwkv = _rwkv(z_rwkv, b, s, shift_mu[0], decay_up[0], decay_bias[0], iclr_up[0], iclr_bias[0],
                   gate_up[0], k_k[0], k_a[0], r_k[0], lnx_w[0], lnx_b[0])
    x1, h2, route, plen_f = _outproj_router(y_attn, y_rwkv, x2, w_out[0].astype(BF16), norm2_gain[0],
                                            router_w[0], router_b[0])
    out = _moe_and_final_norm(h2, route, plen_f, x1, expert_w1[0], expert_b1[0], expert_w2[0],
                              expert_b2[0], final_norm_gain)
    return out.reshape(b, s, d)
```

```python
import functools

import jax
import jax.numpy as jnp
from jax import lax
from jax.experimental import pallas as pl
from jax.experimental.pallas import tpu as pltpu

F32 = jnp.float32
BF16 = jnp.bfloat16
I32 = jnp.int32

D_MODEL = 1024
HEADS = 8
HEAD_DIM = 64
ATTN_WIDTH = HEADS * HEAD_DIM
RWKV_WIDTH = HEADS * HEAD_DIM
DECAY_LORA = 64
ICLR_LORA = 64
GATE_LORA = 128
RWKV_IN = 3 * RWKV_WIDTH + DECAY_LORA + ICLR_LORA + GATE_LORA
IN_WIDTH = 3 * ATTN_WIDTH + RWKV_IN
N_EXPERTS = 32
TOP_K = 4
SWIGLU_ALPHA = 1.702
SWIGLU_LIMIT = 7.0
NORM_EPS = 1e-5
GROUPNORM_EPS = 64e-5
L2_EPS = 1e-12

LANES = 128
SUBLANES = 8
VMEM_LIMIT = 56 * 1024 * 1024
NEG_INF = float("-inf")


def _dot(a, b):
    return jnp.dot(a, b, preferred_element_type=F32)


def _dot_nt(a, b):
    return lax.dot_general(a, b, (((1,), (1,)), ((), ())), preferred_element_type=F32)


def _params(sem, vmem=VMEM_LIMIT):
    return pltpu.CompilerParams(dimension_semantics=sem, vmem_limit_bytes=vmem)


def _inproj_kernel(x_ref, g_ref, w_ref, za_ref, zr_ref):
    x = x_ref[...]
    ms = jnp.mean(x * x, axis=-1, keepdims=True)
    h = (x * lax.rsqrt(ms + NORM_EPS) * g_ref[...]).astype(BF16)
    na = za_ref.shape[1]
    za_ref[...] = _dot(h, w_ref[:, :na])
    zr_ref[...] = _dot(h, w_ref[:, na:])


def _inproj(x2, gain, w_in_bf16, tm=512):
    n, d = x2.shape
    na = 3 * ATTN_WIDTH
    return pl.pallas_call(
        _inproj_kernel,
        grid=(n // tm,),
        in_specs=[
            pl.BlockSpec((tm, d), lambda i: (i, 0)),
            pl.BlockSpec((1, d), lambda i: (0, 0)),
            pl.BlockSpec((d, IN_WIDTH), lambda i: (0, 0)),
        ],
        out_specs=[
            pl.BlockSpec((tm, na), lambda i: (i, 0)),
            pl.BlockSpec((tm, RWKV_IN), lambda i: (i, 0)),
        ],
        out_shape=[
            jax.ShapeDtypeStruct((n, na), F32),
            jax.ShapeDtypeStruct((n, RWKV_IN), F32),
        ],
        compiler_params=_params(("parallel",)),
        name="inproj",
    )(x2, gain.reshape(1, d), w_in_bf16)


ATT_BLK = 128
ATT_DIL = 16
ATT_BLOCKS_PER_ITER = 4
ATT_CLASSES_PER_ITER = 4
LOG2_E = 1.4426950408889634


def _attn_kernel(q_ref, k_ref, v_ref, g_ref, o_ref,
                 kp_ref, vp_ref, m1_ref, l1_ref, a1_ref, of_ref):
    seq = q_ref.shape[0]
    nblk = seq // ATT_BLK
    scale = HEAD_DIM ** -0.5 * LOG2_E
    lane = lax.broadcasted_iota(I32, (1, LANES), 1)
    head_masks = (lane < HEAD_DIM, lane >= HEAD_DIM)

    for r in range(ATT_DIL):
        pb = ((r % 4) * 4 + r // 4) * ATT_BLK
        kp_ref[pb:pb + ATT_BLK, :] = k_ref[pl.ds(r, ATT_BLK, stride=ATT_DIL), :].astype(BF16)
        vp_ref[pb:pb + ATT_BLK, :] = v_ref[pl.ds(r, ATT_BLK, stride=ATT_DIL), :].astype(BF16)

    ii = lax.broadcasted_iota(I32, (ATT_BLK, ATT_BLK), 0)
    jj = lax.broadcasted_iota(I32, (ATT_BLK, ATT_BLK), 1)

    def pack_heads(vals):
        return jnp.where(head_masks[0], vals[0], vals[1])

    cur_ok = jj <= ii
    prev_band = jj >= ii

    def blk_group(g, carry):
        base = g * ATT_BLOCKS_PER_ITER
        rows = [pl.ds(pl.multiple_of((base + u) * ATT_BLK, ATT_BLK), ATT_BLK)
                for u in range(ATT_BLOCKS_PER_ITER)]
        p0 = pl.ds(pl.multiple_of(jnp.maximum(base - 1, 0) * ATT_BLK, ATT_BLK), ATT_BLK)
        qs = [q_ref[rw, :] * scale for rw in rows]
        ks = [k_ref[p0, :].astype(BF16)] + [k_ref[rw, :].astype(BF16) for rw in rows]
        vs = [v_ref[p0, :].astype(BF16)] + [v_ref[rw, :].astype(BF16) for rw in rows]
        prev_ok = [prev_band & (g > 0)] + [prev_band] * (ATT_BLOCKS_PER_ITER - 1)
        chains = [(u, h) for u in range(ATT_BLOCKS_PER_ITER) for h in range(2)]
        qh = [jnp.where(head_masks[h], qs[u], 0.0).astype(BF16) for u, h in chains]
        sc = [jnp.where(cur_ok, _dot_nt(q, ks[u + 1]), NEG_INF) for q, (u, h) in zip(qh, chains)]
        sp = [jnp.where(prev_ok[u], _dot_nt(q, ks[u]), NEG_INF) for q, (u, h) in zip(qh, chains)]
        m = [jnp.max(jnp.maximum(a, b), axis=-1, keepdims=True) for a, b in zip(sc, sp)]
        pc = [jnp.exp2(a - mm) for a, mm in zip(sc, m)]
        pp = [jnp.exp2(b - mm) for b, mm in zip(sp, m)]
        ls = [jnp.sum(a + b, axis=-1, keepdims=True) for a, b in zip(pc, pp)]
        acc = [_dot(a.astype(BF16), vs[u + 1]) + _dot(b.astype(BF16), vs[u])
               for a, b, (u, h) in zip(pc, pp, chains)]
        for u, rw in enumerate(rows):
            m1_ref[rw, :] = pack_heads(m[2 * u:2 * u + 2])
            l1_ref[rw, :] = pack_heads(ls[2 * u:2 * u + 2])
            a1_ref[rw, :] = pack_heads(acc[2 * u:2 * u + 2])
        return carry

    lax.fori_loop(0, nblk // ATT_BLOCKS_PER_ITER, blk_group, 0)

    nk = 4 * ATT_BLK
    qi = lax.broadcasted_iota(I32, (ATT_BLK, nk), 0)
    kc_i = lax.broadcasted_iota(I32, (ATT_BLK, nk), 1)
    k_hi = kc_i >> 7
    k_pos = kc_i & (ATT_BLK - 1)
    gain = g_ref[...]

    def res_group(g, carry):
        per_hi = 4 // ATT_CLASSES_PER_ITER
        r_hi = g // per_hi
        rs = [4 * r_hi + (g % per_hi) * ATT_CLASSES_PER_ITER + u for u in range(ATT_CLASSES_PER_ITER)]
        rows = [pl.ds(r, ATT_BLK, stride=ATT_DIL) for r in rs]
        qr = [q_ref[rw, :] * scale for rw in rows]
        k0 = [pl.ds(pl.multiple_of((r & 3) * nk, nk), nk) for r in rs]
        kc = [kp_ref[kk, :] for kk in k0]
        vc = [vp_ref[kk, :] for kk in k0]
        delta = ATT_DIL * (qi - k_pos) + 4 * (r_hi - k_hi)
        in4 = (delta >= 0) & (delta <= 4 * ATT_BLK)
        in16 = (k_hi == r_hi) & (delta >= 0)
        bias = jnp.where(in4 & in16, 1.0, jnp.where(in4 | in16, 0.0, NEG_INF))
        chains = [(u, h) for u in range(ATT_CLASSES_PER_ITER) for h in range(2)]
        qh = [jnp.where(head_masks[h], qr[u], 0.0).astype(BF16) for u, h in chains]
        s = [_dot_nt(q, kc[u]) + bias for q, (u, h) in zip(qh, chains)]
        m = [jnp.max(a, axis=-1, keepdims=True) for a in s]
        e = [jnp.exp2(a - mm) for a, mm in zip(s, m)]
        ls = [jnp.sum(a, axis=-1, keepdims=True) for a in e]
        acc = [_dot(a.astype(BF16), vc[u]) for a, (u, h) in zip(e, chains)]
        for u, rw in enumerate(rows):
            m2, l2, a2 = (pack_heads(x[2 * u:2 * u + 2]) for x in (m, ls, acc))
            m1 = m1_ref[rw, :]
            mx = jnp.maximum(m1, m2)
            w1 = jnp.exp2(m1 - mx)
            w2 = jnp.exp2(m2 - mx)
            den = l1_ref[rw, :] * w1 + l2 * w2
            o = (a1_ref[rw, :] * w1 + a2 * w2) / den
            o2 = o * o
            msq = pack_heads([jnp.sum(jnp.where(hm, o2, 0.0), axis=-1, keepdims=True)
                              for hm in head_masks])
            of_ref[rw, :] = o * lax.rsqrt(msq * (1.0 / HEAD_DIM) + NORM_EPS) * gain
        return carry

    lax.fori_loop(0, ATT_DIL // ATT_CLASSES_PER_ITER, res_group, 0)
    o_ref[...] = of_ref[...].astype(o_ref.dtype)


def _attention(z_attn, out_gain, batch, seq):
    assert seq == ATT_BLK * ATT_DIL
    pairs = ATTN_WIDTH // LANES
    z3 = z_attn.reshape(batch, seq, 3 * ATTN_WIDTH)
    blk = lambda off: pl.BlockSpec((None, seq, LANES), lambda b, p: (b, 0, off + p))
    out = pl.pallas_call(
        _attn_kernel,
        grid=(batch, pairs),
        in_specs=[blk(0), blk(pairs), blk(2 * pairs),
                  pl.BlockSpec((1, LANES), lambda b, p: (0, p))],
        out_specs=pl.BlockSpec((None, seq, LANES), lambda b, p: (b, 0, p)),
        out_shape=jax.ShapeDtypeStruct((batch, seq, ATTN_WIDTH), BF16),
        scratch_shapes=[
            pltpu.VMEM((seq, LANES), BF16), pltpu.VMEM((seq, LANES), BF16),
            pltpu.VMEM((seq, LANES), F32), pltpu.VMEM((seq, LANES), F32),
            pltpu.VMEM((seq, LANES), F32), pltpu.VMEM((seq, LANES), F32),
        ],
        compiler_params=_params(("parallel", "parallel")),
        name="dilated_attn",
    )(z3, z3, z3, out_gain.reshape(1, ATTN_WIDTH))
    return out.reshape(batch * seq, ATTN_WIDTH)


WKV_CHUNK = 64
WKV_SEQS_PER_STEP = 4


def _split3(x):
    h = x.astype(BF16)
    r1 = x - h.astype(F32)
    m = r1.astype(BF16)
    l = (r1 - m.astype(F32)).astype(BF16)
    return h, m, l


def _seg_sum(x, seg):
    xb = x.astype(BF16)
    return jnp.concatenate(
        [_dot(xb[:, p * LANES:(p + 1) * LANES], seg) for p in range(x.shape[1] // LANES)], axis=1)


def _softplus(y):
    return jnp.maximum(y, 0.0) + jnp.log(1.0 + jnp.exp(-jnp.abs(y)))


def _sigmoid(y):
    return 1.0 / (1.0 + jnp.exp(-y))


def _rwkv_kernel(z_ref, mu_ref, wl_ref, db_ref, ib_ref, gu_ref, kk_ref, ka_ref, rk_ref,
                 lnw_ref, lnb_ref, o_ref, st_ref, prev_ref):
    nseq, cs, _ = z_ref.shape
    w = RWKV_WIDTH
    pairs = w // LANES
    n2 = 2 * cs
    assert n2 == LANES

    @pl.when(pl.program_id(1) == 0)
    def _():
        st_ref[...] = jnp.zeros_like(st_ref)
        prev_ref[...] = jnp.zeros_like(prev_ref)

    lane = lax.broadcasted_iota(I32, (1, LANES), 1)
    head_a = lane < HEAD_DIM
    li = lax.broadcasted_iota(I32, (LANES, LANES), 0)
    lj = lax.broadcasted_iota(I32, (LANES, LANES), 1)
    seg = ((li >> 6) == (lj >> 6)).astype(BF16)
    ti = lax.broadcasted_iota(I32, (cs, cs), 0)
    tj = lax.broadcasted_iota(I32, (cs, cs), 1)
    tri = (ti >= tj).astype(BF16)
    strict = li < lj
    incl = li <= lj
    eye = (li == lj).astype(F32)
    eye_b = eye.astype(BF16)
    sel = ((lax.broadcasted_iota(I32, (cs, LANES), 1) & (cs - 1))
           == lax.broadcasted_iota(I32, (cs, LANES), 0)).astype(BF16)
    row = lax.broadcasted_iota(I32, (cs, 1), 0)

    def stack2(x):
        return jnp.concatenate([jnp.where(head_a, x, 0.0), jnp.where(head_a, 0.0, x)], axis=0).astype(BF16)

    seqs = []
    for i in range(nseq):
        z = z_ref[i]
        zprev = jnp.where(row == 0, prev_ref[i, 0:1, :], pltpu.roll(z, 1, axis=0))
        prev_ref[i, 0:1, :] = z[cs - 1:cs, :]
        zs = z + (zprev - z) * mu_ref[...]
        r_ = zs[:, 0:w]
        k_ = zs[:, w:2 * w]
        v_ = zs[:, 2 * w:3 * w]
        lora_in = zs[:, 3 * w:3 * w + LANES]
        gate_in = zs[:, 3 * w + LANES:]
        xl = jnp.where(head_a, jnp.tanh(lora_in), lora_in).astype(BF16)
        pre = _dot(xl, wl_ref[...])
        wlog = -_softplus(-(pre[:, :w] + db_ref[...])) - 0.5
        lw = -jnp.exp(wlog)
        rate = _sigmoid(pre[:, w:] + ib_ref[...])
        gate = _dot(_sigmoid(gate_in).astype(BF16), gu_ref[...])
        kk = k_ * kk_ref[...]
        kk = kk / jnp.maximum(jnp.sqrt(_seg_sum(kk * kk, seg)), L2_EPS)
        kp = k_ * (1.0 + (rate - 1.0) * ka_ref[...])
        bn = kk * rate
        cum = sum(_dot(tri, t) for t in _split3(lw))
        cend = cum[cs - 1:cs, :]
        e_inv = jnp.exp(-cum)
        e_end = jnp.exp(cend - cum)
        seqs.append(dict(
            r=r_, v=v_, kp=kp, gate=gate, g_end=jnp.exp(cend),
            at=-kk * jnp.exp(cum - lw), rt=r_ * jnp.exp(cum), bt=bn * e_inv, kt=kp * e_inv,
            bp=bn * e_end, kpe=kp * e_end))

    chains = [(i, p) for i in range(nseq) for p in range(pairs)]
    sl = lambda p: slice(p * LANES, (p + 1) * LANES)
    ar = [jnp.concatenate([stack2(seqs[i]["at"][:, sl(p)]), stack2(seqs[i]["rt"][:, sl(p)])], axis=0)
          for i, p in chains]
    bk = [jnp.concatenate([stack2(seqs[i]["bt"][:, sl(p)]), stack2(seqs[i]["kt"][:, sl(p)])], axis=0)
          for i, p in chains]
    g1 = [_dot_nt(b, a) for b, a in zip(bk, ar)]
    ab_t = [jnp.where(strict, g[:n2, :n2], 0.0) for g in g1]
    rb_t = [jnp.where(incl, g[:n2, n2:], 0.0).astype(BF16) for g in g1]
    akrk = [jnp.concatenate([jnp.where(strict, g[n2:, :n2], 0.0), jnp.where(incl, g[n2:, n2:], 0.0)],
                            axis=1).astype(BF16) for g in g1]

    tt = [eye + jnp.where((li >> 1) == (lj >> 1), a, 0.0) for a in ab_t]
    s = 2
    while s < cs:
        sh = s.bit_length() - 1
        off = ((li >> (sh + 1)) == (lj >> (sh + 1))) & ((li >> sh) != (lj >> sh))
        tb = [t.astype(BF16) for t in tt]
        ta = [_dot(t, jnp.where(off, a, 0.0).astype(BF16)).astype(BF16) for t, a in zip(tb, ab_t)]
        tt = [t + _dot(x, b) for t, x, b in zip(tt, ta, tb)]
        s *= 2
    tt = [t.astype(BF16) for t in tt]

    v_t = [_dot_nt(eye_b, stack2(seqs[i]["v"][:, sl(p)])).astype(BF16) for i, p in chains]
    s0 = [st_ref[i, p] for i, p in chains]
    g2 = [_dot_nt(s.astype(BF16), a) for s, a in zip(s0, ar)]
    g3 = [_dot(v, m) for v, m in zip(v_t, akrk)]
    u_t = [_dot((a[:, :n2] + b[:, :n2]).astype(BF16), t).astype(BF16) for a, b, t in zip(g2, g3, tt)]
    y_t = [a[:, n2:] + b[:, n2:] + _dot(u, m) for a, b, u, m in zip(g2, g3, u_t, rb_t)]
    for c, (i, p) in enumerate(chains):
        bkp = jnp.concatenate([stack2(seqs[i]["bp"][:, sl(p)]), stack2(seqs[i]["kpe"][:, sl(p)])], axis=0)
        st_ref[i, p] = s0[c] * seqs[i]["g_end"][:, sl(p)] + _dot(
            jnp.concatenate([u_t[c], v_t[c]], axis=1), bkp)
    ys = [_dot_nt(sel, y.astype(BF16)) for y in y_t]

    inv_n = 1.0 / HEAD_DIM
    for i in range(nseq):
        q = seqs[i]
        y = jnp.concatenate(ys[i * pairs:(i + 1) * pairs], axis=1)
        mean = _seg_sum(y, seg) * inv_n
        yc = y - mean
        var = _seg_sum(yc * yc, seg) * inv_n
        y = yc * lax.rsqrt(var + GROUPNORM_EPS) * lnw_ref[...] + lnb_ref[...]
        y = y + _seg_sum(q["r"] * q["kp"] * rk_ref[...], seg) * q["v"]
        o_ref[i] = (y * q["gate"]).astype(o_ref.dtype)


def _rwkv(z_rwkv, batch, seq, shift_mu, decay_up, decay_bias, iclr_up, iclr_bias, gate_up,
          k_k, k_a, r_k, lnx_w, lnx_b):
    cs = WKV_CHUNK
    w = RWKV_WIDTH
    nb = WKV_SEQS_PER_STEP if batch % WKV_SEQS_PER_STEP == 0 else 1
    z3 = z_rwkv.reshape(batch, seq, RWKV_IN)
    wl = jnp.zeros((LANES, 2 * w), F32)
    wl = wl.at[:DECAY_LORA, :w].set(decay_up).at[DECAY_LORA:, w:].set(iclr_up).astype(BF16)
    row = lambda a: a.reshape(1, -1).astype(F32)
    full = lambda shape: pl.BlockSpec(shape, lambda b, c: (0,) * len(shape))
    out = pl.pallas_call(
        _rwkv_kernel,
        grid=(batch // nb, seq // cs),
        in_specs=[
            pl.BlockSpec((nb, cs, RWKV_IN), lambda b, c: (b, c, 0)),
            full((1, RWKV_IN)), full((LANES, 2 * w)), full((1, w)), full((1, w)),
            full((GATE_LORA, w)), full((1, w)), full((1, w)), full((1, w)), full((1, w)), full((1, w)),
        ],
        out_specs=pl.BlockSpec((nb, cs, w), lambda b, c: (b, c, 0)),
        out_shape=jax.ShapeDtypeStruct((batch, seq, w), BF16),
        scratch_shapes=[
            pltpu.VMEM((nb, w // LANES, LANES, LANES), F32),
            pltpu.VMEM((nb, SUBLANES, RWKV_IN), F32),
        ],
        compiler_params=_params(("parallel", "arbitrary")),
        name="rwkv7",
    )(z3, row(shift_mu), wl, row(decay_bias), row(iclr_bias), gate_up.astype(BF16),
      row(k_k), row(k_a), row(r_k), row(lnx_w), row(lnx_b))
    return out.reshape(batch * seq, w)


ROUTE_TILE = 256
RUN_ALIGN = SUBLANES
TILE_ROWS = 1280
EXPERT_BLOCK = 512
EXPERT_SPLIT = 2
assert TILE_ROWS >= TOP_K * ROUTE_TILE + N_EXPERTS * (RUN_ALIGN - 1) and TILE_ROWS % LANES == 0
HIGHEST = lax.Precision.HIGHEST


def _router_kernel(ya_ref, yr_ref, x_ref, wo_ref, g_ref, rw_ref, rb_ref,
                   x1_ref, h_ref, route_ref, plen_ref):
    t = x_ref.shape[0]
    half = ya_ref.shape[1]
    x1 = x_ref[...] + _dot(ya_ref[...], wo_ref[:half, :]) + _dot(yr_ref[...], wo_ref[half:, :])
    x1_ref[...] = x1
    ms = jnp.mean(x1 * x1, axis=-1, keepdims=True)
    h = x1 * lax.rsqrt(ms + NORM_EPS) * g_ref[...]
    h_hi = h.astype(BF16)
    h_ref[...] = h_hi

    h_lo = (h - h_hi.astype(F32)).astype(BF16)
    both = _dot(h_hi, rw_ref[...])
    logits = both[:, :LANES] + both[:, LANES:] + _dot(h_lo, rw_ref[:, :LANES]) + rb_ref[...]
    lane = lax.broadcasted_iota(I32, (t, LANES), 1)
    lane_f = lane.astype(F32)
    lg = jnp.where(lane < N_EXPERTS, logits, NEG_INF)
    vals, hots = [], []
    for _ in range(TOP_K):
        m = jnp.max(lg, axis=-1, keepdims=True)
        idx = jnp.min(jnp.where(lg == m, lane_f, float(LANES)), axis=-1, keepdims=True)
        hot = lane_f == idx
        vals.append(m)
        hots.append(hot)
        lg = jnp.where(hot, NEG_INF, lg)
    exps = [jnp.exp(v - vals[0]) for v in vals]
    den = exps[0] + exps[1] + exps[2] + exps[3]

    multi = sum(hh.astype(F32) for hh in hots)
    ti = lax.broadcasted_iota(I32, (t, t), 0)
    tj = lax.broadcasted_iota(I32, (t, t), 1)
    before = _dot((tj < ti).astype(BF16), multi.astype(BF16))
    counts = jnp.sum(multi, axis=0, keepdims=True)
    padded = jnp.ceil(counts * (1.0 / RUN_ALIGN)) * RUN_ALIGN
    li = lax.broadcasted_iota(I32, (LANES, LANES), 0)
    lj = lax.broadcasted_iota(I32, (LANES, LANES), 1)
    run_start = jnp.dot(jnp.broadcast_to(padded, (SUBLANES, LANES)), (li < lj).astype(F32),
                        precision=HIGHEST, preferred_element_type=F32)[0:1, :]
    pos = run_start + before

    route = jnp.zeros((t, LANES), F32)
    for k in range(TOP_K):
        pos_k = jnp.sum(jnp.where(hots[k], pos, 0.0), axis=-1, keepdims=True)
        route = route + jnp.where(lane == TOP_K + k, exps[k] / den, 0.0)
        route = route + jnp.where(lane == 2 * TOP_K + k, pos_k, 0.0)
    route_ref[...] = route
    plen_ref[...] = jnp.broadcast_to(padded, (SUBLANES, LANES))


def _outproj_router(y_attn, y_rwkv, x2, w_out_bf16, gain, router_w, router_b):
    n, d = x2.shape
    t = ROUTE_TILE
    nt = n // t
    half = y_attn.shape[1]
    rw = jnp.zeros((d, LANES), F32).at[:, :N_EXPERTS].set(router_w)
    rw_hi = rw.astype(BF16)
    rw = jnp.concatenate([rw_hi, (rw - rw_hi.astype(F32)).astype(BF16)], axis=1)
    rb = jnp.zeros((1, LANES), F32).at[0, :N_EXPERTS].set(router_b)
    full = lambda shape: pl.BlockSpec(shape, lambda i: (0,) * len(shape))
    return pl.pallas_call(
        _router_kernel,
        grid=(nt,),
        in_specs=[
            pl.BlockSpec((t, half), lambda i: (i, 0)),
            pl.BlockSpec((t, half), lambda i: (i, 0)),
            pl.BlockSpec((t, d), lambda i: (i, 0)),
            full((2 * half, d)), full((1, d)), full((d, 2 * LANES)), full((1, LANES)),
        ],
        out_specs=[
            pl.BlockSpec((t, d), lambda i: (i, 0)),
            pl.BlockSpec((t, d), lambda i: (i, 0)),
            pl.BlockSpec((t, LANES), lambda i: (i, 0)),
            pl.BlockSpec((None, SUBLANES, LANES), lambda i: (i, 0, 0)),
        ],
        out_shape=[
            jax.ShapeDtypeStruct((n, d), F32),
            jax.ShapeDtypeStruct((n, d), BF16),
            jax.ShapeDtypeStruct((n, LANES), F32),
            jax.ShapeDtypeStruct((nt, SUBLANES, LANES), F32),
        ],
        compiler_params=_params(("parallel",)),
        name="outproj_router",
    )(y_attn, y_rwkv, x2, w_out_bf16, gain.reshape(1, d), rw, rb)


def _run_layout(plen_f):
    plen = plen_f[:, 0, :N_EXPERTS].astype(I32)
    tot = jnp.sum(plen, axis=0)
    reg = (tot + EXPERT_BLOCK - 1) // EXPERT_BLOCK * EXPERT_BLOCK
    reg_end = jnp.cumsum(reg)
    estart = reg_end - reg
    goff = estart[None, :] + jnp.cumsum(plen, axis=0) - plen
    return plen, goff, estart + tot, reg - tot, reg_end


RUN_BITS = tuple(1 << b for b in range(ROUTE_TILE.bit_length() - 1, RUN_ALIGN.bit_length() - 2, -1))
TAIL_BITS = tuple(b for b in RUN_BITS if b < EXPERT_BLOCK)
TOTAL_BITS = tuple(1 << b for b in range(TILE_ROWS.bit_length() - 1, RUN_ALIGN.bit_length() - 2, -1))
assert RUN_BITS[0] == ROUTE_TILE and RUN_BITS[-1] == RUN_ALIGN and TOTAL_BITS[-1] == RUN_ALIGN


def _start_run_copies(plen_ref, goff_ref, tile, local_ref, hbm_ref, sem, to_hbm):
    def body(e, lo):
        n = plen_ref[tile * N_EXPERTS + e]
        g = goff_ref[tile * N_EXPERTS + e]
        for bit in RUN_BITS:
            @pl.when((n & bit) != 0)
            def _():
                o = n & (-2 * bit)
                loc = local_ref.at[pl.ds(pl.multiple_of(lo + o, RUN_ALIGN), bit), :]
                glob = hbm_ref.at[pl.ds(pl.multiple_of(g + o, RUN_ALIGN), bit), :]
                if to_hbm:
                    pltpu.make_async_copy(loc, glob, sem).start()
                else:
                    pltpu.make_async_copy(glob, loc, sem).start()
        return lo + n
    lax.fori_loop(0, N_EXPERTS, body, 0)


def _wait_rows(total, local_ref, hbm_ref, sem, to_hbm):
    for bit in TOTAL_BITS:
        @pl.when((total & bit) != 0)
        def _():
            loc = local_ref.at[pl.ds(0, bit), :]
            glob = hbm_ref.at[pl.ds(0, bit), :]
            if to_hbm:
                pltpu.make_async_copy(loc, glob, sem).wait()
            else:
                pltpu.make_async_copy(glob, loc, sem).wait()


def _dispatch_kernel(plen_ref, goff_ref, ttot_ref, tstart_ref, tlen_ref, h_ref, route_ref, xs_ref,
                     buf_ref, zero_ref, sem_ref):
    t = pl.program_id(0)
    nt = pl.num_programs(0)
    slot = t & 1
    other = 1 - slot
    tt = h_ref.shape[0]

    pick = (lax.broadcasted_iota(I32, (SUBLANES, LANES), 1)
            == lax.broadcasted_iota(I32, (SUBLANES, LANES), 0) + 2 * TOP_K).astype(F32)
    pos_t = lax.dot_general(pick, route_ref[...], (((1,), (1,)), ((), ())),
                            precision=HIGHEST, preferred_element_type=F32).astype(I32)
    rows = lax.broadcasted_iota(I32, (TILE_ROWS, tt), 0)
    hit = jnp.zeros((TILE_ROWS, tt), F32)
    for k in range(TOP_K):
        hit = jnp.where(rows == pos_t[k:k + 1, :], 1.0, hit)
    buf_ref[slot] = _dot(hit.astype(BF16), h_ref[...])
    _start_run_copies(plen_ref, goff_ref, t, buf_ref.at[slot], xs_ref, sem_ref.at[slot], True)

    @pl.when(t > 0)
    def _():
        _wait_rows(ttot_ref[t - 1], buf_ref.at[other], xs_ref, sem_ref.at[other], True)

    @pl.when(t == nt - 1)
    def _():
        _wait_rows(ttot_ref[t], buf_ref.at[slot], xs_ref, sem_ref.at[slot], True)
        zero_ref[...] = jnp.zeros_like(zero_ref)
        for wait in (False, True):
            def body(e, c):
                n = tlen_ref[e]
                g = tstart_ref[e]
                for bit in TAIL_BITS:
                    @pl.when((n & bit) != 0)
                    def _():
                        o = n & (-2 * bit)
                        cp = pltpu.make_async_copy(
                            zero_ref.at[pl.ds(0, bit), :],
                            xs_ref.at[pl.ds(pl.multiple_of(g + o, RUN_ALIGN), bit), :], sem_ref.at[2])
                        if wait:
                            cp.wait()
                        else:
                            cp.start()
                return c
            lax.fori_loop(0, N_EXPERTS, body, 0)

            zrows = zero_ref.shape[0]

            def slack(i, c):
                g = tstart_ref[N_EXPERTS] + i * zrows
                cp = pltpu.make_async_copy(
                    zero_ref, xs_ref.at[pl.ds(pl.multiple_of(g, RUN_ALIGN), zrows), :], sem_ref.at[2])
                if wait:
                    cp.wait()
                else:
                    cp.start()
                return c
            lax.fori_loop(0, tlen_ref[N_EXPERTS] // zrows, slack, 0)


def _dispatch(h_bf16, route, plen, goff, tstart, tlen, total_rows):
    n, d = h_bf16.shape
    t = ROUTE_TILE
    grid_spec = pltpu.PrefetchScalarGridSpec(
        num_scalar_prefetch=5,
        grid=(n // t,),
        in_specs=[
            pl.BlockSpec((t, d), lambda i, *_: (i, 0)),
            pl.BlockSpec((t, LANES), lambda i, *_: (i, 0)),
        ],
        out_specs=pl.BlockSpec(memory_space=pl.ANY),
        scratch_shapes=[
            pltpu.VMEM((2, TILE_ROWS, d), F32),
            pltpu.VMEM((EXPERT_BLOCK // 2, d), F32),
            pltpu.SemaphoreType.DMA((3,)),
        ],
    )
    return pl.pallas_call(
        _dispatch_kernel,
        grid_spec=grid_spec,
        out_shape=jax.ShapeDtypeStruct((total_rows, d), F32),
        compiler_params=_params(("arbitrary",)),
        name="moe_dispatch",
    )(plen.reshape(-1), goff.reshape(-1), jnp.sum(plen, axis=1), tstart, tlen, h_bf16, route)


def _experts_kernel(bexp_ref, nvalid_ref, x_ref, w1_ref, b1_ref, w2_ref, b2_ref, y_ref,
                    w1b_ref, w2b_ref):
    j = pl.program_id(0)
    valid = j < nvalid_ref[0]
    changed = (j == 0) | (bexp_ref[j] != bexp_ref[jnp.maximum(j - 1, 0)])
    f = w2_ref.shape[0]

    @pl.when(valid & changed)
    def _():
        w1b_ref[...] = w1_ref[...].astype(BF16)
        w2b_ref[...] = w2_ref[...].astype(BF16)

    @pl.when(valid)
    def _():
        rows = x_ref.shape[0] // EXPERT_SPLIT
        parts = [slice(i * rows, (i + 1) * rows) for i in range(EXPERT_SPLIT)]
        us = [_dot(x_ref[r, :].astype(BF16), w1b_ref[...]) + b1_ref[...] for r in parts]
        acts = []
        for u in us:
            glu = jnp.minimum(u[:, :f], SWIGLU_LIMIT)
            lin = jnp.clip(u[:, f:], -SWIGLU_LIMIT, SWIGLU_LIMIT)
            acts.append((glu * _sigmoid(SWIGLU_ALPHA * glu) * (lin + 1.0)).astype(BF16))
        for r, act in zip(parts, acts):
            y_ref[r, :] = _dot(act, w2b_ref[...]) + b2_ref[...]

    @pl.when(jnp.logical_not(valid))
    def _():
        y_ref[...] = jnp.zeros_like(y_ref)


def _experts(xs, block_expert, nvalid, w1, b1, w2, b2):
    rows, d = xs.shape
    bm = EXPERT_BLOCK
    f2 = w1.shape[2]
    f = w2.shape[1]
    grid_spec = pltpu.PrefetchScalarGridSpec(
        num_scalar_prefetch=2,
        grid=(rows // bm,),
        in_specs=[
            pl.BlockSpec((bm, d), lambda j, be, nv: (jnp.minimum(j, jnp.maximum(nv[0] - 1, 0)), 0)),
            pl.BlockSpec((None, d, f2), lambda j, be, nv: (be[j], 0, 0)),
            pl.BlockSpec((None, 1, f2), lambda j, be, nv: (be[j], 0, 0)),
            pl.BlockSpec((None, f, d), lambda j, be, nv: (be[j], 0, 0)),
            pl.BlockSpec((None, 1, d), lambda j, be, nv: (be[j], 0, 0)),
        ],
        out_specs=pl.BlockSpec((bm, d), lambda j, be, nv: (j, 0)),
        scratch_shapes=[pltpu.VMEM((d, f2), BF16), pltpu.VMEM((f, d), BF16)],
    )
    e = w1.shape[0]
    return pl.pallas_call(
        _experts_kernel,
        grid_spec=grid_spec,
        out_shape=jax.ShapeDtypeStruct((rows, d), F32),
        compiler_params=_params(("arbitrary",)),
        name="moe_experts",
    )(block_expert, nvalid, xs, w1, b1.reshape(e, 1, f2), w2, b2.reshape(e, 1, d))


def _combine_kernel(plen_ref, goff_ref, ttot_ref, route_ref, x1_ref, g_ref, ys_ref, o_ref,
                    buf_ref, sem_ref):
    t = pl.program_id(0)
    nt = pl.num_programs(0)
    slot = t & 1
    other = 1 - slot

    @pl.when(t == 0)
    def _():
        buf_ref[...] = jnp.zeros_like(buf_ref)
        _start_run_copies(plen_ref, goff_ref, t, buf_ref.at[0], ys_ref, sem_ref.at[0], False)

    @pl.when(t + 1 < nt)
    def _():
        _start_run_copies(plen_ref, goff_ref, t + 1, buf_ref.at[other], ys_ref, sem_ref.at[other], False)

    route = route_ref[...]
    tt = route.shape[0]
    col = lax.broadcasted_iota(I32, (tt, TILE_ROWS), 1)
    wts = jnp.zeros((tt, TILE_ROWS), F32)
    for k in range(TOP_K):
        pos_k = route[:, 2 * TOP_K + k:2 * TOP_K + k + 1].astype(I32)
        wts = jnp.where(col == pos_k, route[:, TOP_K + k:TOP_K + k + 1], wts)

    _wait_rows(ttot_ref[t], buf_ref.at[slot], ys_ref, sem_ref.at[slot], False)
    x2 = x1_ref[...] + _dot(wts.astype(BF16), buf_ref[slot].astype(BF16))
    ms = jnp.mean(x2 * x2, axis=-1, keepdims=True)
    o_ref[...] = x2 * lax.rsqrt(ms + NORM_EPS) * g_ref[...]


def _combine(ys, route, x1, gain, plen, goff):
    n, d = x1.shape
    t = ROUTE_TILE
    grid_spec = pltpu.PrefetchScalarGridSpec(
        num_scalar_prefetch=3,
        grid=(n // t,),
        in_specs=[
            pl.BlockSpec((t, LANES), lambda i, *_: (i, 0)),
            pl.BlockSpec((t, d), lambda i, *_: (i, 0)),
            pl.BlockSpec((1, d), lambda i, *_: (0, 0)),
            pl.BlockSpec(memory_space=pl.ANY),
        ],
        out_specs=pl.BlockSpec((t, d), lambda i, *_: (i, 0)),
        scratch_shapes=[
            pltpu.VMEM((2, TILE_ROWS, d), F32),
            pltpu.SemaphoreType.DMA((2,)),
        ],
    )
    return pl.pallas_call(
        _combine_kernel,
        grid_spec=grid_spec,
        out_shape=jax.ShapeDtypeStruct((n, d), F32),
        compiler_params=_params(("arbitrary",)),
        name="moe_combine",
    )(plen.reshape(-1), goff.reshape(-1), jnp.sum(plen, axis=1), route, x1, gain.reshape(1, d), ys)


def _moe_rows(n):
    nt = n // ROUTE_TILE
    worst = TOP_K * n + nt * N_EXPERTS * (RUN_ALIGN - 1) + N_EXPERTS * (EXPERT_BLOCK - RUN_ALIGN)
    return -(-worst // EXPERT_BLOCK) * EXPERT_BLOCK


def _moe_and_final_norm(h_bf16, route, plen_f, x1, w1, b1, w2, b2, final_gain):
    n = x1.shape[0]
    rows = _moe_rows(n)
    plen, goff, tstart, tlen, reg_end = _run_layout(plen_f)
    nblocks = rows // EXPERT_BLOCK
    starts = jnp.arange(nblocks, dtype=I32) * EXPERT_BLOCK
    nvalid = (reg_end[-1] // EXPERT_BLOCK).astype(I32).reshape(1)
    bexp = jnp.sum((reg_end[None, :] <= starts[:, None]).astype(I32), axis=1)
    last = jnp.take(bexp, jnp.maximum(nvalid[0] - 1, 0))
    bexp = jnp.where(starts < reg_end[-1], bexp, last)
    tstart = jnp.concatenate([tstart, reg_end[-1:]]).astype(I32)
    tlen = jnp.concatenate([tlen, rows - reg_end[-1:]]).astype(I32)
    xs = _dispatch(h_bf16, route, plen, goff, tstart, tlen, rows)
    ys = _experts(xs, bexp, nvalid, w1, b1, w2, b2)
    return _combine(ys, route, x1, final_gain, plen, goff)


def kernel(x, norm1_gain, w_in, shift_mu, decay_up, decay_bias, iclr_up, iclr_bias, gate_up, k_k, k_a, r_k, lnx_w, lnx_b, attn_out_gain, w_out, norm2_gain, router_w, router_b, expert_w1, expert_b1, expert_w2, expert_b2, final_norm_gain):
    b, s, d = x.shape
    x2 = x.reshape(b * s, d)
    z_attn, z_rwkv = _inproj(x2, norm1_gain[0], w_in[0].astype(BF16))
    y_attn = _attention(z_attn, attn_out_gain[0], b, s)
    y_rwkv = _rwkv(z_rwkv, b, s, shift_mu[0], decay_up[0], decay_bias[0], iclr_up[0], iclr_bias[0],
                   gate_up[0], k_k[0], k_a[0], r_k[0], lnx_w[0], lnx_b[0])
    x1, h2, route, plen_f = _outproj_router(y_attn, y_rwkv, x2, w_out[0].astype(BF16), norm2_gain[0],
                                            router_w[0], router_b[0])
    out = _moe_and_final_norm(h2, route, plen_f, x1, expert_w1[0], expert_b1[0], expert_w2[0],
                              expert_b2[0], final_norm_gain)
    return out.reshape(b, s, d)
```

```python
import jax
import jax.numpy as jnp
from jax import lax
from jax.experimental import pallas as pl
from jax.experimental.pallas import tpu as pltpu

F32 = jnp.float32
BF16 = jnp.bfloat16
I32 = jnp.int32

D_MODEL = 1024
HEADS = 8
HEAD_DIM = 64
ATTN_WIDTH = HEADS * HEAD_DIM
RWKV_WIDTH = HEADS * HEAD_DIM
DECAY_LORA = 64
ICLR_LORA = 64
GATE_LORA = 128
RWKV_IN = 3 * RWKV_WIDTH + DECAY_LORA + ICLR_LORA + GATE_LORA
IN_WIDTH = 3 * ATTN_WIDTH + RWKV_IN
N_EXPERTS = 32
TOP_K = 4
SWIGLU_ALPHA = 1.702
SWIGLU_LIMIT = 7.0
NORM_EPS = 1e-5
GROUPNORM_EPS = 64e-5
L2_EPS = 1e-12

LANES = 128
SUBLANES = 8
VMEM_LIMIT = 56 * 1024 * 1024
NEG_INF = float("-inf")


def _dot(a, b):
    return jnp.dot(a, b, preferred_element_type=F32)


def _dot_nt(a, b):
    return lax.dot_general(a, b, (((1,), (1,)), ((), ())), preferred_element_type=F32)


def _params(sem, vmem=VMEM_LIMIT):
    return pltpu.CompilerParams(dimension_semantics=sem, vmem_limit_bytes=vmem)


def _inproj_kernel(x_ref, g_ref, w_ref, za_ref, zr_ref):
    x = x_ref[...]
    ms = jnp.mean(x * x, axis=-1, keepdims=True)
    h = (x * lax.rsqrt(ms + NORM_EPS) * g_ref[...]).astype(BF16)
    na = za_ref.shape[1]
    za_ref[...] = _dot(h, w_ref[:, :na])
    zr_ref[...] = _dot(h, w_ref[:, na:])


def _inproj(x2, gain, w_in_bf16, tm=512):
    n, d = x2.shape
    na = 3 * ATTN_WIDTH
    return pl.pallas_call(
        _inproj_kernel,
        grid=(n // tm,),
        in_specs=[
            pl.BlockSpec((tm, d), lambda i: (i, 0)),
            pl.BlockSpec((1, d), lambda i: (0, 0)),
            pl.BlockSpec((d, IN_WIDTH), lambda i: (0, 0)),
        ],
        out_specs=[
            pl.BlockSpec((tm, na), lambda i: (i, 0)),
            pl.BlockSpec((tm, RWKV_IN), lambda i: (i, 0)),
        ],
        out_shape=[
            jax.ShapeDtypeStruct((n, na), F32),
            jax.ShapeDtypeStruct((n, RWKV_IN), F32),
        ],
        compiler_params=_params(("parallel",)),
        name="inproj",
    )(x2, gain.reshape(1, d), w_in_bf16)


ATT_BLK = 128
ATT_DIL = 16
ATT_BLOCKS_PER_ITER = 4
ATT_CLASSES_PER_ITER = 4
LOG2_E = 1.4426950408889634


def _attn_kernel(q_ref, k_ref, v_ref, g_ref, o_ref,
                 kp_ref, vp_ref, m1_ref, l1_ref, a1_ref, of_ref):
    seq = q_ref.shape[0]
    nblk = seq // ATT_BLK
    scale = HEAD_DIM ** -0.5 * LOG2_E
    lane = lax.broadcasted_iota(I32, (1, LANES), 1)
    head_masks = (lane < HEAD_DIM, lane >= HEAD_DIM)

    for r in range(ATT_DIL):
        pb = ((r % 4) * 4 + r // 4) * ATT_BLK
        kp_ref[pb:pb + ATT_BLK, :] = k_ref[pl.ds(r, ATT_BLK, stride=ATT_DIL), :].astype(BF16)
        vp_ref[pb:pb + ATT_BLK, :] = v_ref[pl.ds(r, ATT_BLK, stride=ATT_DIL), :].astype(BF16)

    ii = lax.broadcasted_iota(I32, (ATT_BLK, ATT_BLK), 0)
    jj = lax.broadcasted_iota(I32, (ATT_BLK, ATT_BLK), 1)

    def pack_heads(vals):
        return jnp.where(head_masks[0], vals[0], vals[1])

    cur_ok = jj <= ii
    prev_band = jj >= ii

    def blk_group(g, carry):
        base = g * ATT_BLOCKS_PER_ITER
        rows = [pl.ds(pl.multiple_of((base + u) * ATT_BLK, ATT_BLK), ATT_BLK)
                for u in range(ATT_BLOCKS_PER_ITER)]
        p0 = pl.ds(pl.multiple_of(jnp.maximum(base - 1, 0) * ATT_BLK, ATT_BLK), ATT_BLK)
        qs = [q_ref[rw, :] * scale for rw in rows]
        ks = [k_ref[p0, :].astype(BF16)] + [k_ref[rw, :].astype(BF16) for rw in rows]
        vs = [v_ref[p0, :].astype(BF16)] + [v_ref[rw, :].astype(BF16) for rw in rows]
        prev_ok = [prev_band & (g > 0)] + [prev_band] * (ATT_BLOCKS_PER_ITER - 1)
        chains = [(u, h) for u in range(ATT_BLOCKS_PER_ITER) for h in range(2)]
        qh = [jnp.where(head_masks[h], qs[u], 0.0).astype(BF16) for u, h in chains]
        sc = [jnp.where(cur_ok, _dot_nt(q, ks[u + 1]), NEG_INF) for q, (u, h) in zip(qh, chains)]
        sp = [jnp.where(prev_ok[u], _dot_nt(q, ks[u]), NEG_INF) for q, (u, h) in zip(qh, chains)]
        m = [jnp.max(jnp.maximum(a, b), axis=-1, keepdims=True) for a, b in zip(sc, sp)]
        pc = [jnp.exp2(a - mm) for a, mm in zip(sc, m)]
        pp = [jnp.exp2(b - mm) for b, mm in zip(sp, m)]
        ls = [jnp.sum(a + b, axis=-1, keepdims=True) for a, b in zip(pc, pp)]
        acc = [_dot(a.astype(BF16), vs[u + 1]) + _dot(b.astype(BF16), vs[u])
               for a, b, (u, h) in zip(pc, pp, chains)]
        for u, rw in enumerate(rows):
            m1_ref[rw, :] = pack_heads(m[2 * u:2 * u + 2])
            l1_ref[rw, :] = pack_heads(ls[2 * u:2 * u + 2])
            a1_ref[rw, :] = pack_heads(acc[2 * u:2 * u + 2])
        return carry

    lax.fori_loop(0, nblk // ATT_BLOCKS_PER_ITER, blk_group, 0)

    nk = 4 * ATT_BLK
    qi = lax.broadcasted_iota(I32, (ATT_BLK, nk), 0)
    kc_i = lax.broadcasted_iota(I32, (ATT_BLK, nk), 1)
    k_hi = kc_i >> 7
    k_pos = kc_i & (ATT_BLK - 1)
    gain = g_ref[...]

    def res_group(g, carry):
        per_hi = 4 // ATT_CLASSES_PER_ITER
        r_hi = g // per_hi
        rs = [4 * r_hi + (g % per_hi) * ATT_CLASSES_PER_ITER + u for u in range(ATT_CLASSES_PER_ITER)]
        rows = [pl.ds(r, ATT_BLK, stride=ATT_DIL) for r in rs]
        qr = [q_ref[rw, :] * scale for rw in rows]
        k0 = [pl.ds(pl.multiple_of((r & 3) * nk, nk), nk) for r in rs]
        kc = [kp_ref[kk, :] for kk in k0]
        vc = [vp_ref[kk, :] for kk in k0]
        delta = ATT_DIL * (qi - k_pos) + 4 * (r_hi - k_hi)
        in4 = (delta >= 0) & (delta <= 4 * ATT_BLK)
        in16 = (k_hi == r_hi) & (delta >= 0)
        bias = jnp.where(in4 & in16, 1.0, jnp.where(in4 | in16, 0.0, NEG_INF))
        chains = [(u, h) for u in range(ATT_CLASSES_PER_ITER) for h in range(2)]
        qh = [jnp.where(head_masks[h], qr[u], 0.0).astype(BF16) for u, h in chains]
        s = [_dot_nt(q, kc[u]) + bias for q, (u, h) in zip(qh, chains)]
        m = [jnp.max(a, axis=-1, keepdims=True) for a in s]
        e = [jnp.exp2(a - mm) for a, mm in zip(s, m)]
        ls = [jnp.sum(a, axis=-1, keepdims=True) for a in e]
        acc = [_dot(a.astype(BF16), vc[u]) for a, (u, h) in zip(e, chains)]
        for u, rw in enumerate(rows):
            m2, l2, a2 = (pack_heads(x[2 * u:2 * u + 2]) for x in (m, ls, acc))
            m1 = m1_ref[rw, :]
            mx = jnp.maximum(m1, m2)
            w1 = jnp.exp2(m1 - mx)
            w2 = jnp.exp2(m2 - mx)
            den = l1_ref[rw, :] * w1 + l2 * w2
            o = (a1_ref[rw, :] * w1 + a2 * w2) / den
            o2 = o * o
            msq = pack_heads([jnp.sum(jnp.where(hm, o2, 0.0), axis=-1, keepdims=True)
                              for hm in head_masks])
            of_ref[rw, :] = o * lax.rsqrt(msq * (1.0 / HEAD_DIM) + NORM_EPS) * gain
        return carry

    lax.fori_loop(0, ATT_DIL // ATT_CLASSES_PER_ITER, res_group, 0)
    o_ref[...] = of_ref[...].astype(o_ref.dtype)


def _attention(z_attn, out_gain, batch, seq):
    assert seq == ATT_BLK * ATT_DIL
    pairs = ATTN_WIDTH // LANES
    z3 = z_attn.reshape(batch, seq, 3 * ATTN_WIDTH)
    blk = lambda off: pl.BlockSpec((None, seq, LANES), lambda b, p: (b, 0, off + p))
    out = pl.pallas_call(
        _attn_kernel,
        grid=(batch, pairs),
        in_specs=[blk(0), blk(pairs), blk(2 * pairs),
                  pl.BlockSpec((1, LANES), lambda b, p: (0, p))],
        out_specs=pl.BlockSpec((None, seq, LANES), lambda b, p: (b, 0, p)),
        out_shape=jax.ShapeDtypeStruct((batch, seq, ATTN_WIDTH), BF16),
        scratch_shapes=[
            pltpu.VMEM((seq, LANES), BF16), pltpu.VMEM((seq, LANES), BF16),
            pltpu.VMEM((seq, LANES), F32), pltpu.VMEM((seq, LANES), F32),
            pltpu.VMEM((seq, LANES), F32), pltpu.VMEM((seq, LANES), F32),
        ],
        compiler_params=_params(("parallel", "parallel")),
        name="dilated_attn",
    )(z3, z3, z3, out_gain.reshape(1, ATTN_WIDTH))
    return out.reshape(batch * seq, ATTN_WIDTH)


WKV_CHUNK = 64
WKV_SEQS_PER_STEP = 4


def _split3(x):
    h = x.astype(BF16)
    r1 = x - h.astype(F32)
    m = r1.astype(BF16)
    l = (r1 - m.astype(F32)).astype(BF16)
    return h, m, l


def _seg_sum(x, seg):
    xb = x.astype(BF16)
    return jnp.concatenate(
        [_dot(xb[:, p * LANES:(p + 1) * LANES], seg) for p in range(x.shape[1] // LANES)], axis=1)


def _softplus(y):
    return jnp.maximum(y, 0.0) + jnp.log(1.0 + jnp.exp(-jnp.abs(y)))


def _sigmoid(y):
    return 1.0 / (1.0 + jnp.exp(-y))


def _rwkv_kernel(z_ref, mu_ref, wl_ref, db_ref, ib_ref, gu_ref, kk_ref, ka_ref, rk_ref,
                 lnw_ref, lnb_ref, o_ref, st_ref, prev_ref):
    nseq, cs, _ = z_ref.shape
    w = RWKV_WIDTH
    pairs = w // LANES
    n2 = 2 * cs
    assert n2 == LANES

    @pl.when(pl.program_id(1) == 0)
    def _():
        st_ref[...] = jnp.zeros_like(st_ref)
        prev_ref[...] = jnp.zeros_like(prev_ref)

    lane = lax.broadcasted_iota(I32, (1, LANES), 1)
    head_a = lane < HEAD_DIM
    li = lax.broadcasted_iota(I32, (LANES, LANES), 0)
    lj = lax.broadcasted_iota(I32, (LANES, LANES), 1)
    seg = ((li >> 6) == (lj >> 6)).astype(BF16)
    ti = lax.broadcasted_iota(I32, (cs, cs), 0)
    tj = lax.broadcasted_iota(I32, (cs, cs), 1)
    tri = (ti >= tj).astype(BF16)
    strict = li < lj
    incl = li <= lj
    eye = (li == lj).astype(F32)
    row = lax.broadcasted_iota(I32, (cs, 1), 0)

    def stack2(x, dtype=BF16):
        return jnp.concatenate([jnp.where(head_a, x, 0.0), jnp.where(head_a, 0.0, x)], axis=0).astype(dtype)

    seqs = []
    for i in range(nseq):
        z = z_ref[i]
        zprev = jnp.where(row == 0, prev_ref[i, 0:1, :], pltpu.roll(z, 1, axis=0))
        prev_ref[i, 0:1, :] = z[cs - 1:cs, :]
        zs = z + (zprev - z) * mu_ref[...]
        r_ = zs[:, 0:w]
        k_ = zs[:, w:2 * w]
        v_ = zs[:, 2 * w:3 * w]
        lora_in = zs[:, 3 * w:3 * w + LANES]
        gate_in = zs[:, 3 * w + LANES:]
        xl = jnp.where(head_a, jnp.tanh(lora_in), lora_in).astype(BF16)
        pre = _dot(xl, wl_ref[...])
        wlog = -_softplus(-(pre[:, :w] + db_ref[...])) - 0.5
        lw = -jnp.exp(wlog)
        rate = _sigmoid(pre[:, w:] + ib_ref[...])
        gate = _dot(_sigmoid(gate_in).astype(BF16), gu_ref[...])
        kk = k_ * kk_ref[...]
        kk = kk / jnp.maximum(jnp.sqrt(_seg_sum(kk * kk, seg)), L2_EPS)
        kp = k_ * (1.0 + (rate - 1.0) * ka_ref[...])
        bn = kk * rate
        cum = sum(_dot(tri, t) for t in _split3(lw))
        cend = cum[cs - 1:cs, :]
        e_inv = jnp.exp(-cum)
        e_end = jnp.exp(cend - cum)
        seqs.append(dict(
            r=r_, v=v_, kp=kp, gate=gate, g_end=jnp.exp(cend),
            at=-kk * jnp.exp(cum - lw), rt=r_ * jnp.exp(cum), bt=bn * e_inv, kt=kp * e_inv,
            bp=bn * e_end, kpe=kp * e_end))

    chains = [(i, p) for i in range(nseq) for p in range(pairs)]
    sl = lambda p: slice(p * LANES, (p + 1) * LANES)
    ar = [jnp.concatenate([stack2(seqs[i]["at"][:, sl(p)]), stack2(seqs[i]["rt"][:, sl(p)])], axis=0)
          for i, p in chains]
    bk = [jnp.concatenate([stack2(seqs[i]["bt"][:, sl(p)]), stack2(seqs[i]["kt"][:, sl(p)])], axis=0)
          for i, p in chains]
    g1 = [_dot_nt(b, a) for b, a in zip(bk, ar)]
    ab_t = [jnp.where(strict, g[:n2, :n2], 0.0) for g in g1]
    rb_t = [jnp.where(incl, g[:n2, n2:], 0.0).astype(BF16) for g in g1]
    akrk = [jnp.concatenate([jnp.where(strict, g[n2:, :n2], 0.0), jnp.where(incl, g[n2:, n2:], 0.0)],
                            axis=1).astype(BF16) for g in g1]

    tt = [eye + jnp.where((li >> 1) == (lj >> 1), a, 0.0) for a in ab_t]
    s = 2
    while s < cs:
        sh = s.bit_length() - 1
        off = ((li >> (sh + 1)) == (lj >> (sh + 1))) & ((li >> sh) != (lj >> sh))
        tb = [t.astype(BF16) for t in tt]
        ta = [_dot(t, jnp.where(off, a, 0.0).astype(BF16)).astype(BF16) for t, a in zip(tb, ab_t)]
        tt = [t + _dot(x, b) for t, x, b in zip(tt, ta, tb)]
        s *= 2
    tt = [t.astype(BF16) for t in tt]

    v_t = [stack2(seqs[i]["v"][:, sl(p)], F32).T.astype(BF16) for i, p in chains]
    s0 = [st_ref[i, p] for i, p in chains]
    g2 = [_dot_nt(s.astype(BF16), a) for s, a in zip(s0, ar)]
    g3 = [_dot(v, m) for v, m in zip(v_t, akrk)]
    u_t = [_dot((a[:, :n2] + b[:, :n2]).astype(BF16), t).astype(BF16) for a, b, t in zip(g2, g3, tt)]
    y_t = [a[:, n2:] + b[:, n2:] + _dot(u, m) for a, b, u, m in zip(g2, g3, u_t, rb_t)]
    for c, (i, p) in enumerate(chains):
        bkp = jnp.concatenate([stack2(seqs[i]["bp"][:, sl(p)]), stack2(seqs[i]["kpe"][:, sl(p)])], axis=0)
        st_ref[i, p] = s0[c] * seqs[i]["g_end"][:, sl(p)] + _dot(
            jnp.concatenate([u_t[c], v_t[c]], axis=1), bkp)
    ys = [jnp.where(head_a, yt[:cs, :], yt[cs:, :]) for yt in (y.T for y in y_t)]

    inv_n = 1.0 / HEAD_DIM
    for i in range(nseq):
        q = seqs[i]
        y = jnp.concatenate(ys[i * pairs:(i + 1) * pairs], axis=1)
        mean = _seg_sum(y, seg) * inv_n
        yc = y - mean
        var = _seg_sum(yc * yc, seg) * inv_n
        y = yc * lax.rsqrt(var + GROUPNORM_EPS) * lnw_ref[...] + lnb_ref[...]
        y = y + _seg_sum(q["r"] * q["kp"] * rk_ref[...], seg) * q["v"]
        o_ref[i] = (y * q["gate"]).astype(o_ref.dtype)


def _rwkv(z_rwkv, batch, seq, shift_mu, decay_up, decay_bias, iclr_up, iclr_bias, gate_up,
          k_k, k_a, r_k, lnx_w, lnx_b):
    cs = WKV_CHUNK
    w = RWKV_WIDTH
    nb = WKV_SEQS_PER_STEP if batch % WKV_SEQS_PER_STEP == 0 else 1
    z3 = z_rwkv.reshape(batch, seq, RWKV_IN)
    wl = jnp.zeros((LANES, 2 * w), F32)
    wl = wl.at[:DECAY_LORA, :w].set(decay_up).at[DECAY_LORA:, w:].set(iclr_up).astype(BF16)
    row = lambda a: a.reshape(1, -1).astype(F32)
    full = lambda shape: pl.BlockSpec(shape, lambda b, c: (0,) * len(shape))
    out = pl.pallas_call(
        _rwkv_kernel,
        grid=(batch // nb, seq // cs),
        in_specs=[
            pl.BlockSpec((nb, cs, RWKV_IN), lambda b, c: (b, c, 0)),
            full((1, RWKV_IN)), full((LANES, 2 * w)), full((1, w)), full((1, w)),
            full((GATE_LORA, w)), full((1, w)), full((1, w)), full((1, w)), full((1, w)), full((1, w)),
        ],
        out_specs=pl.BlockSpec((nb, cs, w), lambda b, c: (b, c, 0)),
        out_shape=jax.ShapeDtypeStruct((batch, seq, w), BF16),
        scratch_shapes=[
            pltpu.VMEM((nb, w // LANES, LANES, LANES), F32),
            pltpu.VMEM((nb, SUBLANES, RWKV_IN), F32),
        ],
        compiler_params=_params(("parallel", "arbitrary")),
        name="rwkv7",
    )(z3, row(shift_mu), wl, row(decay_bias), row(iclr_bias), gate_up.astype(BF16),
      row(k_k), row(k_a), row(r_k), row(lnx_w), row(lnx_b))
    return out.reshape(batch * seq, w)


ROUTE_TILE = 256
RUN_ALIGN = SUBLANES
TILE_ROWS = 1280
EXPERT_BLOCK = 512
EXPERT_SPLIT = 2
assert TILE_ROWS >= TOP_K * ROUTE_TILE + N_EXPERTS * (RUN_ALIGN - 1) and TILE_ROWS % LANES == 0
HIGHEST = lax.Precision.HIGHEST


def _router_kernel(ya_ref, yr_ref, x_ref, wo_ref, g_ref, rwt_ref, rbt_ref,
                   x1_ref, h_ref, route_ref, routet_ref, plen_ref):
    t = x_ref.shape[0]
    ne = N_EXPERTS
    half = ya_ref.shape[1]
    x1 = x_ref[...] + _dot(ya_ref[...], wo_ref[:half, :]) + _dot(yr_ref[...], wo_ref[half:, :])
    x1_ref[...] = x1
    ms = jnp.mean(x1 * x1, axis=-1, keepdims=True)
    h = x1 * lax.rsqrt(ms + NORM_EPS) * g_ref[...]
    h_hi = h.astype(BF16)
    h_ref[...] = h_hi

    h_lo = (h - h_hi.astype(F32)).astype(BF16)
    both = _dot_nt(rwt_ref[...], h_hi)
    lg = both[:ne, :] + both[ne:, :] + _dot_nt(rwt_ref[:ne, :], h_lo) + rbt_ref[...]
    e_f = lax.broadcasted_iota(I32, (ne, t), 0).astype(F32)
    vals, hots = [], []
    for _ in range(TOP_K):
        m = jnp.max(lg, axis=0, keepdims=True)
        idx = jnp.min(jnp.where(lg == m, e_f, float(ne)), axis=0, keepdims=True)
        hot = e_f == idx
        vals.append(m)
        hots.append(hot)
        lg = jnp.where(hot, NEG_INF, lg)
    exps = [jnp.exp(v - vals[0]) for v in vals]
    den = exps[0] + exps[1] + exps[2] + exps[3]

    multi = sum(hh.astype(F32) for hh in hots)
    ti = lax.broadcasted_iota(I32, (t, t), 0)
    tj = lax.broadcasted_iota(I32, (t, t), 1)
    before = _dot(multi.astype(BF16), (ti < tj).astype(BF16))
    counts = jnp.sum(multi, axis=1, keepdims=True)
    padded = jnp.ceil(counts * (1.0 / RUN_ALIGN)) * RUN_ALIGN
    ei = lax.broadcasted_iota(I32, (ne, ne), 0)
    ej = lax.broadcasted_iota(I32, (ne, ne), 1)
    run_start = jnp.dot((ej < ei).astype(F32), jnp.broadcast_to(padded, (ne, LANES)),
                        precision=HIGHEST, preferred_element_type=F32)[:, 0:1]
    pos = run_start + before

    blank = jnp.zeros((TOP_K, t), F32)
    route_t = jnp.concatenate(
        [blank] + [e / den for e in exps]
        + [jnp.sum(jnp.where(hh, pos, 0.0), axis=0, keepdims=True) for hh in hots] + [blank], axis=0)
    routet_ref[...] = route_t
    fill = jnp.zeros((LANES - route_t.shape[0], LANES), F32)
    for c in range(t // LANES):
        tile = jnp.concatenate([route_t[:, c * LANES:(c + 1) * LANES], fill], axis=0)
        route_ref[c * LANES:(c + 1) * LANES, :] = tile.T

    multi_b = jnp.concatenate([multi, jnp.zeros((LANES - ne, t), F32)], axis=0).astype(BF16)
    counts_row = _dot_nt(jnp.ones((2 * SUBLANES, t), BF16), multi_b)[:SUBLANES, :]
    plen_ref[...] = jnp.ceil(counts_row * (1.0 / RUN_ALIGN)) * RUN_ALIGN


def _outproj_router(y_attn, y_rwkv, x2, w_out_bf16, gain, router_w, router_b):
    n, d = x2.shape
    t = ROUTE_TILE
    nt = n // t
    half = y_attn.shape[1]
    rw_hi = router_w.astype(BF16)
    rwt = jnp.concatenate([rw_hi.T, (router_w - rw_hi.astype(F32)).astype(BF16).T], axis=0)
    rbt = jnp.broadcast_to(router_b.astype(F32)[:, None], (N_EXPERTS, t))
    full = lambda shape: pl.BlockSpec(shape, lambda i: (0,) * len(shape))
    return pl.pallas_call(
        _router_kernel,
        grid=(nt,),
        in_specs=[
            pl.BlockSpec((t, half), lambda i: (i, 0)),
            pl.BlockSpec((t, half), lambda i: (i, 0)),
            pl.BlockSpec((t, d), lambda i: (i, 0)),
            full((2 * half, d)), full((1, d)), full((2 * N_EXPERTS, d)), full((N_EXPERTS, t)),
        ],
        out_specs=[
            pl.BlockSpec((t, d), lambda i: (i, 0)),
            pl.BlockSpec((t, d), lambda i: (i, 0)),
            pl.BlockSpec((t, LANES), lambda i: (i, 0)),
            pl.BlockSpec((None, 4 * TOP_K, t), lambda i: (i, 0, 0)),
            pl.BlockSpec((None, SUBLANES, LANES), lambda i: (i, 0, 0)),
        ],
        out_shape=[
            jax.ShapeDtypeStruct((n, d), F32),
            jax.ShapeDtypeStruct((n, d), BF16),
            jax.ShapeDtypeStruct((n, LANES), F32),
            jax.ShapeDtypeStruct((nt, 4 * TOP_K, t), F32),
            jax.ShapeDtypeStruct((nt, SUBLANES, LANES), F32),
        ],
        compiler_params=_params(("parallel",)),
        name="outproj_router",
    )(y_attn, y_rwkv, x2, w_out_bf16, gain.reshape(1, d), rwt, rbt)


def _run_layout(plen_f):
    plen = plen_f[:, 0, :N_EXPERTS].astype(I32)
    tot = jnp.sum(plen, axis=0)
    reg = (tot + EXPERT_BLOCK - 1) // EXPERT_BLOCK * EXPERT_BLOCK
    reg_end = jnp.cumsum(reg)
    estart = reg_end - reg
    goff = estart[None, :] + jnp.cumsum(plen, axis=0) - plen
    return plen, goff, estart + tot, reg - tot, reg_end


TAIL_BITS = tuple(1 << b for b in range(EXPERT_BLOCK.bit_length() - 2, RUN_ALIGN.bit_length() - 2, -1))
assert TAIL_BITS[0] == EXPERT_BLOCK // 2 and TAIL_BITS[-1] == RUN_ALIGN


def _start_run_copies(plen_ref, goff_ref, tile, local_ref, hbm_ref, sem, to_hbm):
    def body(e, lo):
        n = plen_ref[tile * N_EXPERTS + e]
        g = goff_ref[tile * N_EXPERTS + e]

        @pl.when(n > 0)
        def _():
            rows = pl.multiple_of(n, RUN_ALIGN)
            loc = local_ref.at[pl.ds(pl.multiple_of(lo, RUN_ALIGN), rows), :]
            glob = hbm_ref.at[pl.ds(pl.multiple_of(g, RUN_ALIGN), rows), :]
            if to_hbm:
                pltpu.make_async_copy(loc, glob, sem).start()
            else:
                pltpu.make_async_copy(glob, loc, sem).start()
        return lo + n
    lax.fori_loop(0, N_EXPERTS, body, 0)


def _wait_rows(total, local_ref, hbm_ref, sem, to_hbm):
    @pl.when(total > 0)
    def _():
        rows = pl.multiple_of(total, RUN_ALIGN)
        loc = local_ref.at[pl.ds(0, rows), :]
        glob = hbm_ref.at[pl.ds(0, rows), :]
        if to_hbm:
            pltpu.make_async_copy(loc, glob, sem).wait()
        else:
            pltpu.make_async_copy(glob, loc, sem).wait()


def _dispatch_kernel(plen_ref, goff_ref, ttot_ref, tstart_ref, tlen_ref, h_ref, routet_ref, xs_ref,
                     buf_ref, zero_ref, sem_ref):
    t = pl.program_id(0)
    nt = pl.num_programs(0)
    slot = t & 1
    other = 1 - slot
    tt = h_ref.shape[0]

    pos_t = routet_ref[2 * TOP_K:3 * TOP_K, :].astype(I32)
    rows = lax.broadcasted_iota(I32, (TILE_ROWS, tt), 0)
    hit = jnp.zeros((TILE_ROWS, tt), F32)
    for k in range(TOP_K):
        hit = jnp.where(rows == pos_t[k:k + 1, :], 1.0, hit)
    buf_ref[slot] = _dot(hit.astype(BF16), h_ref[...])
    _start_run_copies(plen_ref, goff_ref, t, buf_ref.at[slot], xs_ref, sem_ref.at[slot], True)

    @pl.when(t > 0)
    def _():
        _wait_rows(ttot_ref[t - 1], buf_ref.at[other], xs_ref, sem_ref.at[other], True)

    @pl.when(t == nt - 1)
    def _():
        _wait_rows(ttot_ref[t], buf_ref.at[slot], xs_ref, sem_ref.at[slot], True)
        zero_ref[...] = jnp.zeros_like(zero_ref)
        zrows = zero_ref.shape[0]
        for wait in (False, True):
            def body(e, c):
                n = tlen_ref[e]
                g = tstart_ref[e]
                for bit in TAIL_BITS:
                    @pl.when((n & bit) != 0)
                    def _():
                        o = n & (-2 * bit)
                        cp = pltpu.make_async_copy(
                            zero_ref.at[pl.ds(0, bit), :],
                            xs_ref.at[pl.ds(pl.multiple_of(g + o, RUN_ALIGN), bit), :], sem_ref.at[2])
                        if wait:
                            cp.wait()
                        else:
                            cp.start()
                return c
            lax.fori_loop(0, N_EXPERTS, body, 0)

            def slack(i, c):
                g = tstart_ref[N_EXPERTS] + i * zrows
                cp = pltpu.make_async_copy(
                    zero_ref, xs_ref.at[pl.ds(pl.multiple_of(g, RUN_ALIGN), zrows), :], sem_ref.at[2])
                if wait:
                    cp.wait()
                else:
                    cp.start()
                return c
            lax.fori_loop(0, tlen_ref[N_EXPERTS] // zrows, slack, 0)


def _dispatch(h_bf16, route_t, plen, goff, tstart, tlen, total_rows):
    n, d = h_bf16.shape
    t = ROUTE_TILE
    grid_spec = pltpu.PrefetchScalarGridSpec(
        num_scalar_prefetch=5,
        grid=(n // t,),
        in_specs=[
            pl.BlockSpec((t, d), lambda i, *_: (i, 0)),
            pl.BlockSpec((None, 4 * TOP_K, t), lambda i, *_: (i, 0, 0)),
        ],
        out_specs=pl.BlockSpec(memory_space=pl.ANY),
        scratch_shapes=[
            pltpu.VMEM((2, TILE_ROWS, d), F32),
            pltpu.VMEM((EXPERT_BLOCK // 2, d), F32),
            pltpu.SemaphoreType.DMA((3,)),
        ],
    )
    return pl.pallas_call(
        _dispatch_kernel,
        grid_spec=grid_spec,
        out_shape=jax.ShapeDtypeStruct((total_rows, d), F32),
        compiler_params=_params(("arbitrary",)),
        name="moe_dispatch",
    )(plen.reshape(-1), goff.reshape(-1), jnp.sum(plen, axis=1), tstart, tlen, h_bf16, route_t)


def _experts_kernel(bexp_ref, nvalid_ref, x_ref, w1_ref, b1_ref, w2_ref, b2_ref, y_ref,
                    w1b_ref, w2b_ref):
    j = pl.program_id(0)
    valid = j < nvalid_ref[0]
    changed = (j == 0) | (bexp_ref[j] != bexp_ref[jnp.maximum(j - 1, 0)])
    f = w2_ref.shape[0]

    @pl.when(valid & changed)
    def _():
        w1b_ref[...] = w1_ref[...].astype(BF16)
        w2b_ref[...] = w2_ref[...].astype(BF16)

    @pl.when(valid)
    def _():
        rows = x_ref.shape[0] // EXPERT_SPLIT
        parts = [slice(i * rows, (i + 1) * rows) for i in range(EXPERT_SPLIT)]
        us = [_dot(x_ref[r, :].astype(BF16), w1b_ref[...]) + b1_ref[...] for r in parts]
        acts = []
        for u in us:
            glu = jnp.minimum(u[:, :f], SWIGLU_LIMIT)
            lin = jnp.clip(u[:, f:], -SWIGLU_LIMIT, SWIGLU_LIMIT)
            acts.append((glu * _sigmoid(SWIGLU_ALPHA * glu) * (lin + 1.0)).astype(BF16))
        for r, act in zip(parts, acts):
            y_ref[r, :] = _dot(act, w2b_ref[...]) + b2_ref[...]

    @pl.when(jnp.logical_not(valid))
    def _():
        y_ref[...] = jnp.zeros_like(y_ref)


def _experts(xs, block_expert, nvalid, w1, b1, w2, b2):
    rows, d = xs.shape
    bm = EXPERT_BLOCK
    f2 = w1.shape[2]
    f = w2.shape[1]
    grid_spec = pltpu.PrefetchScalarGridSpec(
        num_scalar_prefetch=2,
        grid=(rows // bm,),
        in_specs=[
            pl.BlockSpec((bm, d), lambda j, be, nv: (jnp.minimum(j, jnp.maximum(nv[0] - 1, 0)), 0)),
            pl.BlockSpec((None, d, f2), lambda j, be, nv: (be[j], 0, 0)),
            pl.BlockSpec((None, 1, f2), lambda j, be, nv: (be[j], 0, 0)),
            pl.BlockSpec((None, f, d), lambda j, be, nv: (be[j], 0, 0)),
            pl.BlockSpec((None, 1, d), lambda j, be, nv: (be[j], 0, 0)),
        ],
        out_specs=pl.BlockSpec((bm, d), lambda j, be, nv: (j, 0)),
        scratch_shapes=[pltpu.VMEM((d, f2), BF16), pltpu.VMEM((f, d), BF16)],
    )
    e = w1.shape[0]
    return pl.pallas_call(
        _experts_kernel,
        grid_spec=grid_spec,
        out_shape=jax.ShapeDtypeStruct((rows, d), F32),
        compiler_params=_params(("arbitrary",)),
        name="moe_experts",
    )(block_expert, nvalid, xs, w1, b1.reshape(e, 1, f2), w2, b2.reshape(e, 1, d))


def _combine_kernel(plen_ref, goff_ref, ttot_ref, route_ref, x1_ref, g_ref, ys_ref, o_ref,
                    buf_ref, sem_ref):
    t = pl.program_id(0)
    nt = pl.num_programs(0)
    slot = t & 1
    other = 1 - slot

    @pl.when(t == 0)
    def _():
        buf_ref[...] = jnp.zeros_like(buf_ref)
        _start_run_copies(plen_ref, goff_ref, t, buf_ref.at[0], ys_ref, sem_ref.at[0], False)

    @pl.when(t + 1 < nt)
    def _():
        _start_run_copies(plen_ref, goff_ref, t + 1, buf_ref.at[other], ys_ref, sem_ref.at[other], False)

    route = route_ref[...]
    tt = route.shape[0]
    col = lax.broadcasted_iota(I32, (tt, TILE_ROWS), 1)
    wts = jnp.zeros((tt, TILE_ROWS), F32)
    for k in range(TOP_K):
        pos_k = route[:, 2 * TOP_K + k:2 * TOP_K + k + 1].astype(I32)
        wts = jnp.where(col == pos_k, route[:, TOP_K + k:TOP_K + k + 1], wts)

    _wait_rows(ttot_ref[t], buf_ref.at[slot], ys_ref, sem_ref.at[slot], False)
    x2 = x1_ref[...] + _dot(wts.astype(BF16), buf_ref[slot].astype(BF16))
    ms = jnp.mean(x2 * x2, axis=-1, keepdims=True)
    o_ref[...] = x2 * lax.rsqrt(ms + NORM_EPS) * g_ref[...]


def _combine(ys, route, x1, gain, plen, goff):
    n, d = x1.shape
    t = ROUTE_TILE
    grid_spec = pltpu.PrefetchScalarGridSpec(
        num_scalar_prefetch=3,
        grid=(n // t,),
        in_specs=[
            pl.BlockSpec((t, LANES), lambda i, *_: (i, 0)),
            pl.BlockSpec((t, d), lambda i, *_: (i, 0)),
            pl.BlockSpec((1, d), lambda i, *_: (0, 0)),
            pl.BlockSpec(memory_space=pl.ANY),
        ],
        out_specs=pl.BlockSpec((t, d), lambda i, *_: (i, 0)),
        scratch_shapes=[
            pltpu.VMEM((2, TILE_ROWS, d), F32),
            pltpu.SemaphoreType.DMA((2,)),
        ],
    )
    return pl.pallas_call(
        _combine_kernel,
        grid_spec=grid_spec,
        out_shape=jax.ShapeDtypeStruct((n, d), F32),
        compiler_params=_params(("arbitrary",)),
        name="moe_combine",
    )(plen.reshape(-1), goff.reshape(-1), jnp.sum(plen, axis=1), route, x1, gain.reshape(1, d), ys)


def _moe_rows(n):
    nt = n // ROUTE_TILE
    worst = TOP_K * n + nt * N_EXPERTS * (RUN_ALIGN - 1) + N_EXPERTS * (EXPERT_BLOCK - RUN_ALIGN)
    return -(-worst // EXPERT_BLOCK) * EXPERT_BLOCK


def _moe_and_final_norm(h_bf16, route, route_t, plen_f, x1, w1, b1, w2, b2, final_gain):
    n = x1.shape[0]
    rows = _moe_rows(n)
    plen, goff, tstart, tlen, reg_end = _run_layout(plen_f)
    nblocks = rows // EXPERT_BLOCK
    starts = jnp.arange(nblocks, dtype=I32) * EXPERT_BLOCK
    nvalid = (reg_end[-1] // EXPERT_BLOCK).astype(I32).reshape(1)
    bexp = jnp.sum((reg_end[None, :] <= starts[:, None]).astype(I32), axis=1)
    last = jnp.take(bexp, jnp.maximum(nvalid[0] - 1, 0))
    bexp = jnp.where(starts < reg_end[-1], bexp, last)
    tstart = jnp.concatenate([tstart, reg_end[-1:]]).astype(I32)
    tlen = jnp.concatenate([tlen, rows - reg_end[-1:]]).astype(I32)
    xs = _dispatch(h_bf16, route_t, plen, goff, tstart, tlen, rows)
    ys = _experts(xs, bexp, nvalid, w1, b1, w2, b2)
    return _combine(ys, route, x1, final_gain, plen, goff)


def kernel(x, norm1_gain, w_in, shift_mu, decay_up, decay_bias, iclr_up, iclr_bias, gate_up, k_k, k_a, r_k, lnx_w, lnx_b, attn_out_gain, w_out, norm2_gain, router_w, router_b, expert_w1, expert_b1, expert_w2, expert_b2, final_norm_gain):
    b, s, d = x.shape
    x2 = x.reshape(b * s, d)
    z_attn, z_rwkv = _inproj(x2, norm1_gain[0], w_in[0].astype(BF16))
    y_attn = _attention(z_attn, attn_out_gain[0], b, s)
    y_rwkv = _rwkv(z_rwkv, b, s, shift_mu[0], decay_up[0], decay_bias[0], iclr_up[0], iclr_bias[0],
                   gate_up[0], k_k[0], k_a[0], r_k[0], lnx_w[0], lnx_b[0])
    x1, h2, route, route_t, plen_f = _outproj_router(
        y_attn, y_rwkv, x2, w_out[0].astype(BF16), norm2_gain[0], router_w[0], router_b[0])
    out = _moe_and_final_norm(h2, route, route_t, plen_f, x1, expert_w1[0], expert_b1[0],
                              expert_w2[0], expert_b2[0], final_norm_gain)
    return out.reshape(b, s, d)
```

```python
import jax
import jax.numpy as jnp
from jax import lax
from jax.experimental import pallas as pl
from jax.experimental.pallas import tpu as pltpu

F32 = jnp.float32
BF16 = jnp.bfloat16
I32 = jnp.int32

D_MODEL = 1024
HEADS = 8
HEAD_DIM = 64
ATTN_WIDTH = HEADS * HEAD_DIM
RWKV_WIDTH = HEADS * HEAD_DIM
DECAY_LORA = 64
ICLR_LORA = 64
GATE_LORA = 128
RWKV_IN = 3 * RWKV_WIDTH + DECAY_LORA + ICLR_LORA + GATE_LORA
IN_WIDTH = 3 * ATTN_WIDTH + RWKV_IN
N_EXPERTS = 32
TOP_K = 4
SWIGLU_ALPHA = 1.702
SWIGLU_LIMIT = 7.0
NORM_EPS = 1e-5
GROUPNORM_EPS = 64e-5
L2_EPS = 1e-12

LANES = 128
SUBLANES = 8
VMEM_LIMIT = 56 * 1024 * 1024
NEG_INF = float("-inf")


def _dot(a, b):
    return jnp.dot(a, b, preferred_element_type=F32)


def _dot_nt(a, b):
    return lax.dot_general(a, b, (((1,), (1,)), ((), ())), preferred_element_type=F32)


def _params(sem, vmem=VMEM_LIMIT):
    return pltpu.CompilerParams(dimension_semantics=sem, vmem_limit_bytes=vmem)


def _inproj_kernel(x_ref, g_ref, w_ref, za_ref, zr_ref):
    x = x_ref[...]
    ms = jnp.mean(x * x, axis=-1, keepdims=True)
    h = (x * lax.rsqrt(ms + NORM_EPS) * g_ref[...]).astype(BF16)
    na = za_ref.shape[1]
    za_ref[...] = _dot(h, w_ref[:, :na])
    zr_ref[...] = _dot(h, w_ref[:, na:])


def _inproj(x2, gain, w_in_bf16, tm=512):
    n, d = x2.shape
    na = 3 * ATTN_WIDTH
    return pl.pallas_call(
        _inproj_kernel,
        grid=(n // tm,),
        in_specs=[
            pl.BlockSpec((tm, d), lambda i: (i, 0)),
            pl.BlockSpec((1, d), lambda i: (0, 0)),
            pl.BlockSpec((d, IN_WIDTH), lambda i: (0, 0)),
        ],
        out_specs=[
            pl.BlockSpec((tm, na), lambda i: (i, 0)),
            pl.BlockSpec((tm, RWKV_IN), lambda i: (i, 0)),
        ],
        out_shape=[
            jax.ShapeDtypeStruct((n, na), F32),
            jax.ShapeDtypeStruct((n, RWKV_IN), F32),
        ],
        compiler_params=_params(("parallel",)),
        name="inproj",
    )(x2, gain.reshape(1, d), w_in_bf16)


ATT_BLK = 128
ATT_DIL = 16
ATT_BLOCKS_PER_ITER = 16
ATT_CLASSES_PER_ITER = 4
ATT_CLASSES_PER_GROUP = 4
LOG2_E = 1.4426950408889634


def _attn_kernel(q_ref, k_ref, v_ref, g_ref, o_ref,
                 kp_ref, vp_ref, m1_ref, l1_ref, a1_ref, of_ref):
    seq = q_ref.shape[0]
    nblk = seq // ATT_BLK
    scale = HEAD_DIM ** -0.5 * LOG2_E
    lane = lax.broadcasted_iota(I32, (1, LANES), 1)
    head_masks = (lane < HEAD_DIM, lane >= HEAD_DIM)

    for r in range(ATT_DIL):
        pb = ((r % 4) * 4 + r // 4) * ATT_BLK
        kp_ref[pb:pb + ATT_BLK, :] = k_ref[pl.ds(r, ATT_BLK, stride=ATT_DIL), :].astype(BF16)
        vp_ref[pb:pb + ATT_BLK, :] = v_ref[pl.ds(r, ATT_BLK, stride=ATT_DIL), :].astype(BF16)

    ii = lax.broadcasted_iota(I32, (ATT_BLK, ATT_BLK), 0)
    jj = lax.broadcasted_iota(I32, (ATT_BLK, ATT_BLK), 1)

    def pack_heads(vals):
        return jnp.where(head_masks[0], vals[0], vals[1])

    cur_ok = jj <= ii
    prev_band = jj >= ii

    def blk_group(g, carry):
        base = g * ATT_BLOCKS_PER_ITER
        rows = [pl.ds(pl.multiple_of((base + u) * ATT_BLK, ATT_BLK), ATT_BLK)
                for u in range(ATT_BLOCKS_PER_ITER)]
        p0 = pl.ds(pl.multiple_of(jnp.maximum(base - 1, 0) * ATT_BLK, ATT_BLK), ATT_BLK)
        qs = [q_ref[rw, :] * scale for rw in rows]
        ks = [k_ref[p0, :].astype(BF16)] + [k_ref[rw, :].astype(BF16) for rw in rows]
        vs = [v_ref[p0, :].astype(BF16)] + [v_ref[rw, :].astype(BF16) for rw in rows]
        prev_ok = [prev_band & (g > 0)] + [prev_band] * (ATT_BLOCKS_PER_ITER - 1)
        chains = [(u, h) for u in range(ATT_BLOCKS_PER_ITER) for h in range(2)]
        qh = [jnp.where(head_masks[h], qs[u], 0.0).astype(BF16) for u, h in chains]
        sc = [jnp.where(cur_ok, _dot_nt(q, ks[u + 1]), NEG_INF) for q, (u, h) in zip(qh, chains)]
        sp = [jnp.where(prev_ok[u], _dot_nt(q, ks[u]), NEG_INF) for q, (u, h) in zip(qh, chains)]
        m = [jnp.max(jnp.maximum(a, b), axis=-1, keepdims=True) for a, b in zip(sc, sp)]
        pc = [jnp.exp2(a - mm) for a, mm in zip(sc, m)]
        pp = [jnp.exp2(b - mm) for b, mm in zip(sp, m)]
        ls = [jnp.sum(a + b, axis=-1, keepdims=True) for a, b in zip(pc, pp)]
        acc = [_dot(a.astype(BF16), vs[u + 1]) + _dot(b.astype(BF16), vs[u])
               for a, b, (u, h) in zip(pc, pp, chains)]
        for u, rw in enumerate(rows):
            m1_ref[rw, :] = pack_heads(m[2 * u:2 * u + 2])
            l1_ref[rw, :] = pack_heads(ls[2 * u:2 * u + 2])
            a1_ref[rw, :] = pack_heads(acc[2 * u:2 * u + 2])
        return carry

    lax.fori_loop(0, nblk // ATT_BLOCKS_PER_ITER, blk_group, 0)

    nk = 4 * ATT_BLK
    qi = lax.broadcasted_iota(I32, (ATT_BLK, nk), 0)
    kc_i = lax.broadcasted_iota(I32, (ATT_BLK, nk), 1)
    k_hi = kc_i >> 7
    k_pos = kc_i & (ATT_BLK - 1)
    gain = g_ref[...]

    def res_group(g, carry):
        per_hi = 4 // ATT_CLASSES_PER_ITER
        r_hi = g // per_hi
        rs = [4 * r_hi + (g % per_hi) * ATT_CLASSES_PER_ITER + u for u in range(ATT_CLASSES_PER_ITER)]
        rows = [pl.ds(r, ATT_BLK, stride=ATT_DIL) for r in rs]
        qr = [q_ref[rw, :] * scale for rw in rows]
        k0 = [pl.ds(pl.multiple_of((r & 3) * nk, nk), nk) for r in rs]
        kc = [kp_ref[kk, :] for kk in k0]
        vc = [vp_ref[kk, :] for kk in k0]
        delta = ATT_DIL * (qi - k_pos) + 4 * (r_hi - k_hi)
        in4 = (delta >= 0) & (delta <= 4 * ATT_BLK)
        in16 = (k_hi == r_hi) & (delta >= 0)
        bias = jnp.where(in4 & in16, 1.0, jnp.where(in4 | in16, 0.0, NEG_INF))
        for u0 in range(0, ATT_CLASSES_PER_ITER, ATT_CLASSES_PER_GROUP):
            us = range(u0, u0 + ATT_CLASSES_PER_GROUP)
            chains = [(u, h) for u in us for h in range(2)]
            qh = [jnp.where(head_masks[h], qr[u], 0.0).astype(BF16) for u, h in chains]
            s = [_dot_nt(q, kc[u]) + bias for q, (u, h) in zip(qh, chains)]
            m = [jnp.max(a, axis=-1, keepdims=True) for a in s]
            e = [jnp.exp2(a - mm) for a, mm in zip(s, m)]
            ls = [jnp.sum(a, axis=-1, keepdims=True) for a in e]
            acc = [_dot(a.astype(BF16), vc[u]) for a, (u, h) in zip(e, chains)]
            for j, u in enumerate(us):
                rw = rows[u]
                m2, l2, a2 = (pack_heads(x[2 * j:2 * j + 2]) for x in (m, ls, acc))
                m1 = m1_ref[rw, :]
                mx = jnp.maximum(m1, m2)
                w1 = jnp.exp2(m1 - mx)
                w2 = jnp.exp2(m2 - mx)
                den = l1_ref[rw, :] * w1 + l2 * w2
                o = (a1_ref[rw, :] * w1 + a2 * w2) / den
                o2 = o * o
                msq = pack_heads([jnp.sum(jnp.where(hm, o2, 0.0), axis=-1, keepdims=True)
                                  for hm in head_masks])
                of_ref[rw, :] = o * lax.rsqrt(msq * (1.0 / HEAD_DIM) + NORM_EPS) * gain
        return carry

    lax.fori_loop(0, ATT_DIL // ATT_CLASSES_PER_ITER, res_group, 0)
    o_ref[...] = of_ref[...].astype(o_ref.dtype)


def _attention(z_attn, out_gain, batch, seq):
    assert seq == ATT_BLK * ATT_DIL
    pairs = ATTN_WIDTH // LANES
    z3 = z_attn.reshape(batch, seq, 3 * ATTN_WIDTH)
    blk = lambda off: pl.BlockSpec((None, seq, LANES), lambda b, p: (b, 0, off + p))
    out = pl.pallas_call(
        _attn_kernel,
        grid=(batch, pairs),
        in_specs=[blk(0), blk(pairs), blk(2 * pairs),
                  pl.BlockSpec((1, LANES), lambda b, p: (0, p))],
        out_specs=pl.BlockSpec((None, seq, LANES), lambda b, p: (b, 0, p)),
        out_shape=jax.ShapeDtypeStruct((batch, seq, ATTN_WIDTH), BF16),
        scratch_shapes=[
            pltpu.VMEM((seq, LANES), BF16), pltpu.VMEM((seq, LANES), BF16),
            pltpu.VMEM((seq, LANES), F32), pltpu.VMEM((seq, LANES), F32),
            pltpu.VMEM((seq, LANES), F32), pltpu.VMEM((seq, LANES), F32),
        ],
        compiler_params=_params(("parallel", "parallel")),
        name="dilated_attn",
    )(z3, z3, z3, out_gain.reshape(1, ATTN_WIDTH))
    return out.reshape(batch * seq, ATTN_WIDTH)


WKV_CHUNK = 64
WKV_SEQS_PER_STEP = 4


def _split3(x):
    h = x.astype(BF16)
    r1 = x - h.astype(F32)
    m = r1.astype(BF16)
    l = (r1 - m.astype(F32)).astype(BF16)
    return h, m, l


def _seg_sum(x, seg):
    xb = x.astype(BF16)
    return jnp.concatenate(
        [_dot(xb[:, p * LANES:(p + 1) * LANES], seg) for p in range(x.shape[1] // LANES)], axis=1)


def _softplus(y):
    return jnp.maximum(y, 0.0) + jnp.log(1.0 + jnp.exp(-jnp.abs(y)))


def _sigmoid(y):
    return 1.0 / (1.0 + jnp.exp(-y))


def _rwkv_kernel(z_ref, mu_ref, wl_ref, db_ref, ib_ref, gu_ref, kk_ref, ka_ref, rk_ref,
                 lnw_ref, lnb_ref, o_ref, st_ref, prev_ref):
    nseq, cs, _ = z_ref.shape
    w = RWKV_WIDTH
    pairs = w // LANES
    n2 = 2 * cs
    assert n2 == LANES

    @pl.when(pl.program_id(1) == 0)
    def _():
        st_ref[...] = jnp.zeros_like(st_ref)
        prev_ref[...] = jnp.zeros_like(prev_ref)

    lane = lax.broadcasted_iota(I32, (1, LANES), 1)
    head_a = lane < HEAD_DIM
    li = lax.broadcasted_iota(I32, (LANES, LANES), 0)
    lj = lax.broadcasted_iota(I32, (LANES, LANES), 1)
    seg = ((li >> 6) == (lj >> 6)).astype(BF16)
    ti = lax.broadcasted_iota(I32, (cs, cs), 0)
    tj = lax.broadcasted_iota(I32, (cs, cs), 1)
    tri = (ti >= tj).astype(BF16)
    strict = li < lj
    incl = li <= lj
    eye = (li == lj).astype(F32)
    row = lax.broadcasted_iota(I32, (cs, 1), 0)

    def stack2(x, dtype=BF16):
        return jnp.concatenate([jnp.where(head_a, x, 0.0), jnp.where(head_a, 0.0, x)], axis=0).astype(dtype)

    seqs = []
    for i in range(nseq):
        z = z_ref[i]
        zprev = jnp.where(row == 0, prev_ref[i, 0:1, :], pltpu.roll(z, 1, axis=0))
        prev_ref[i, 0:1, :] = z[cs - 1:cs, :]
        zs = z + (zprev - z) * mu_ref[...]
        r_ = zs[:, 0:w]
        k_ = zs[:, w:2 * w]
        v_ = zs[:, 2 * w:3 * w]
        lora_in = zs[:, 3 * w:3 * w + LANES]
        gate_in = zs[:, 3 * w + LANES:]
        xl = jnp.where(head_a, jnp.tanh(lora_in), lora_in).astype(BF16)
        pre = _dot(xl, wl_ref[...])
        wlog = -_softplus(-(pre[:, :w] + db_ref[...])) - 0.5
        lw = -jnp.exp(wlog)
        rate = _sigmoid(pre[:, w:] + ib_ref[...])
        gate = _dot(_sigmoid(gate_in).astype(BF16), gu_ref[...])
        kk = k_ * kk_ref[...]
        kk = kk / jnp.maximum(jnp.sqrt(_seg_sum(kk * kk, seg)), L2_EPS)
        kp = k_ * (1.0 + (rate - 1.0) * ka_ref[...])
        bn = kk * rate
        cum = sum(_dot(tri, t) for t in _split3(lw))
        cend = cum[cs - 1:cs, :]
        e_inv = jnp.exp(-cum)
        e_end = jnp.exp(cend - cum)
        seqs.append(dict(
            r=r_, v=v_, kp=kp, gate=gate, g_end=jnp.exp(cend),
            at=-kk * jnp.exp(cum - lw), rt=r_ * jnp.exp(cum), bt=bn * e_inv, kt=kp * e_inv,
            bp=bn * e_end, kpe=kp * e_end))

    chains = [(i, p) for i in range(nseq) for p in range(pairs)]
    sl = lambda p: slice(p * LANES, (p + 1) * LANES)
    ar = [jnp.concatenate([stack2(seqs[i]["at"][:, sl(p)]), stack2(seqs[i]["rt"][:, sl(p)])], axis=0)
          for i, p in chains]
    bk = [jnp.concatenate([stack2(seqs[i]["bt"][:, sl(p)]), stack2(seqs[i]["kt"][:, sl(p)])], axis=0)
          for i, p in chains]
    g1 = [_dot_nt(b, a) for b, a in zip(bk, ar)]
    ab_t = [jnp.where(strict, g[:n2, :n2], 0.0) for g in g1]
    rb_t = [jnp.where(incl, g[:n2, n2:], 0.0).astype(BF16) for g in g1]
    akrk = [jnp.concatenate([jnp.where(strict, g[n2:, :n2], 0.0), jnp.where(incl, g[n2:, n2:], 0.0)],
                            axis=1).astype(BF16) for g in g1]

    tt = [eye + jnp.where((li >> 1) == (lj >> 1), a, 0.0) for a in ab_t]
    s = 2
    while s < cs:
        sh = s.bit_length() - 1
        off = ((li >> (sh + 1)) == (lj >> (sh + 1))) & ((li >> sh) != (lj >> sh))
        tb = [t.astype(BF16) for t in tt]
        ta = [_dot(t, jnp.where(off, a, 0.0).astype(BF16)).astype(BF16) for t, a in zip(tb, ab_t)]
        tt = [t + _dot(x, b) for t, x, b in zip(tt, ta, tb)]
        s *= 2
    tt = [t.astype(BF16) for t in tt]

    v_t = [stack2(seqs[i]["v"][:, sl(p)], F32).T.astype(BF16) for i, p in chains]
    s0 = [st_ref[i, p] for i, p in chains]
    g2 = [_dot_nt(s.astype(BF16), a) for s, a in zip(s0, ar)]
    g3 = [_dot(v, m) for v, m in zip(v_t, akrk)]
    u_t = [_dot((a[:, :n2] + b[:, :n2]).astype(BF16), t).astype(BF16) for a, b, t in zip(g2, g3, tt)]
    y_t = [a[:, n2:] + b[:, n2:] + _dot(u, m) for a, b, u, m in zip(g2, g3, u_t, rb_t)]
    for c, (i, p) in enumerate(chains):
        bkp = jnp.concatenate([stack2(seqs[i]["bp"][:, sl(p)]), stack2(seqs[i]["kpe"][:, sl(p)])], axis=0)
        st_ref[i, p] = s0[c] * seqs[i]["g_end"][:, sl(p)] + _dot(
            jnp.concatenate([u_t[c], v_t[c]], axis=1), bkp)
    ys = [jnp.where(head_a, yt[:cs, :], yt[cs:, :]) for yt in (y.T for y in y_t)]

    inv_n = 1.0 / HEAD_DIM
    for i in range(nseq):
        q = seqs[i]
        y = jnp.concatenate(ys[i * pairs:(i + 1) * pairs], axis=1)
        mean = _seg_sum(y, seg) * inv_n
        yc = y - mean
        var = _seg_sum(yc * yc, seg) * inv_n
        y = yc * lax.rsqrt(var + GROUPNORM_EPS) * lnw_ref[...] + lnb_ref[...]
        y = y + _seg_sum(q["r"] * q["kp"] * rk_ref[...], seg) * q["v"]
        o_ref[i] = (y * q["gate"]).astype(o_ref.dtype)


def _rwkv(z_rwkv, batch, seq, shift_mu, decay_up, decay_bias, iclr_up, iclr_bias, gate_up,
          k_k, k_a, r_k, lnx_w, lnx_b):
    cs = WKV_CHUNK
    w = RWKV_WIDTH
    nb = WKV_SEQS_PER_STEP if batch % WKV_SEQS_PER_STEP == 0 else 1
    z3 = z_rwkv.reshape(batch, seq, RWKV_IN)
    wl = jnp.zeros((LANES, 2 * w), F32)
    wl = wl.at[:DECAY_LORA, :w].set(decay_up).at[DECAY_LORA:, w:].set(iclr_up).astype(BF16)
    row = lambda a: a.reshape(1, -1).astype(F32)
    full = lambda shape: pl.BlockSpec(shape, lambda b, c: (0,) * len(shape))
    out = pl.pallas_call(
        _rwkv_kernel,
        grid=(batch // nb, seq // cs),
        in_specs=[
            pl.BlockSpec((nb, cs, RWKV_IN), lambda b, c: (b, c, 0)),
            full((1, RWKV_IN)), full((LANES, 2 * w)), full((1, w)), full((1, w)),
            full((GATE_LORA, w)), full((1, w)), full((1, w)), full((1, w)), full((1, w)), full((1, w)),
        ],
        out_specs=pl.BlockSpec((nb, cs, w), lambda b, c: (b, c, 0)),
        out_shape=jax.ShapeDtypeStruct((batch, seq, w), BF16),
        scratch_shapes=[
            pltpu.VMEM((nb, w // LANES, LANES, LANES), F32),
            pltpu.VMEM((nb, SUBLANES, RWKV_IN), F32),
        ],
        compiler_params=_params(("parallel", "arbitrary")),
        name="rwkv7",
    )(z3, row(shift_mu), wl, row(decay_bias), row(iclr_bias), gate_up.astype(BF16),
      row(k_k), row(k_a), row(r_k), row(lnx_w), row(lnx_b))
    return out.reshape(batch * seq, w)


ROUTE_TILE = 256
RUN_ALIGN = SUBLANES
TILE_ROWS = 1280
EXPERT_BLOCK = 512
EXPERT_SPLIT = 2
ROUTER_TILES_PER_STEP = 4
assert TILE_ROWS >= TOP_K * ROUTE_TILE + N_EXPERTS * (RUN_ALIGN - 1) and TILE_ROWS % LANES == 0
HIGHEST = lax.Precision.HIGHEST


def _router_kernel(ya_ref, yr_ref, x_ref, wo_ref, g_ref, rwt_ref, rbt_ref,
                   x1_ref, h_ref, route_ref, routet_ref, plen_ref):
    t = ROUTE_TILE
    ne = N_EXPERTS
    half = ya_ref.shape[1]
    tiles = range(x_ref.shape[0] // t)
    rows = [slice(s * t, (s + 1) * t) for s in tiles]
    x1 = [x_ref[r, :] + _dot(ya_ref[r, :], wo_ref[:half, :]) + _dot(yr_ref[r, :], wo_ref[half:, :])
          for r in rows]
    h = []
    for r, v in zip(rows, x1):
        x1_ref[r, :] = v
        ms = jnp.mean(v * v, axis=-1, keepdims=True)
        h.append(v * lax.rsqrt(ms + NORM_EPS) * g_ref[...])
    h_hi = [v.astype(BF16) for v in h]
    for r, v in zip(rows, h_hi):
        h_ref[r, :] = v

    h_lo = [(v - vb.astype(F32)).astype(BF16) for v, vb in zip(h, h_hi)]
    both = [_dot_nt(rwt_ref[...], vb) for vb in h_hi]
    lg = [bt[:ne, :] + bt[ne:, :] + _dot_nt(rwt_ref[:ne, :], vl) + rbt_ref[...]
          for bt, vl in zip(both, h_lo)]
    e_f = lax.broadcasted_iota(I32, (ne, t), 0).astype(F32)
    vals = [[] for _ in tiles]
    hots = [[] for _ in tiles]
    for _ in range(TOP_K):
        for s in tiles:
            m = jnp.max(lg[s], axis=0, keepdims=True)
            idx = jnp.min(jnp.where(lg[s] == m, e_f, float(ne)), axis=0, keepdims=True)
            hot = e_f == idx
            vals[s].append(m)
            hots[s].append(hot)
            lg[s] = jnp.where(hot, NEG_INF, lg[s])

    ti = lax.broadcasted_iota(I32, (t, t), 0)
    tj = lax.broadcasted_iota(I32, (t, t), 1)
    earlier = (ti < tj).astype(BF16)
    ei = lax.broadcasted_iota(I32, (ne, ne), 0)
    ej = lax.broadcasted_iota(I32, (ne, ne), 1)
    lower = (ej < ei).astype(F32)
    multi = [sum(hh.astype(F32) for hh in hots[s]) for s in tiles]
    before = [_dot(mm.astype(BF16), earlier) for mm in multi]
    padded = [jnp.ceil(jnp.sum(mm, axis=1, keepdims=True) * (1.0 / RUN_ALIGN)) * RUN_ALIGN
              for mm in multi]
    run_start = [jnp.dot(lower, jnp.broadcast_to(pd, (ne, LANES)), precision=HIGHEST,
                         preferred_element_type=F32)[:, 0:1] for pd in padded]
    ones = jnp.ones((2 * SUBLANES, t), BF16)
    counts_row = [_dot_nt(ones, jnp.concatenate([mm, jnp.zeros((LANES - ne, t), F32)], axis=0).astype(BF16))
                  for mm in multi]

    blank = jnp.zeros((TOP_K, t), F32)
    for s in tiles:
        pos = run_start[s] + before[s]
        exps = [jnp.exp(v - vals[s][0]) for v in vals[s]]
        den = exps[0] + exps[1] + exps[2] + exps[3]
        route_t = jnp.concatenate(
            [blank] + [e / den for e in exps]
            + [jnp.sum(jnp.where(hh, pos, 0.0), axis=0, keepdims=True) for hh in hots[s]] + [blank], axis=0)
        routet_ref[s] = route_t
        fill = jnp.zeros((LANES - route_t.shape[0], LANES), F32)
        for c in range(t // LANES):
            tile = jnp.concatenate([route_t[:, c * LANES:(c + 1) * LANES], fill], axis=0)
            route_ref[s * t + c * LANES:s * t + (c + 1) * LANES, :] = tile.T
        plen_ref[s] = jnp.ceil(counts_row[s][:SUBLANES, :] * (1.0 / RUN_ALIGN)) * RUN_ALIGN


def _outproj_router(y_attn, y_rwkv, x2, w_out_bf16, gain, router_w, router_b):
    n, d = x2.shape
    t = ROUTE_TILE
    nt = n // t
    ns = ROUTER_TILES_PER_STEP if nt % ROUTER_TILES_PER_STEP == 0 else 1
    tm = ns * t
    half = y_attn.shape[1]
    rw_hi = router_w.astype(BF16)
    rwt = jnp.concatenate([rw_hi.T, (router_w - rw_hi.astype(F32)).astype(BF16).T], axis=0)
    rbt = jnp.broadcast_to(router_b.astype(F32)[:, None], (N_EXPERTS, t))
    full = lambda shape: pl.BlockSpec(shape, lambda i: (0,) * len(shape))
    return pl.pallas_call(
        _router_kernel,
        grid=(nt // ns,),
        in_specs=[
            pl.BlockSpec((tm, half), lambda i: (i, 0)),
            pl.BlockSpec((tm, half), lambda i: (i, 0)),
            pl.BlockSpec((tm, d), lambda i: (i, 0)),
            full((2 * half, d)), full((1, d)), full((2 * N_EXPERTS, d)), full((N_EXPERTS, t)),
        ],
        out_specs=[
            pl.BlockSpec((tm, d), lambda i: (i, 0)),
            pl.BlockSpec((tm, d), lambda i: (i, 0)),
            pl.BlockSpec((tm, LANES), lambda i: (i, 0)),
            pl.BlockSpec((ns, 4 * TOP_K, t), lambda i: (i, 0, 0)),
            pl.BlockSpec((ns, SUBLANES, LANES), lambda i: (i, 0, 0)),
        ],
        out_shape=[
            jax.ShapeDtypeStruct((n, d), F32),
            jax.ShapeDtypeStruct((n, d), BF16),
            jax.ShapeDtypeStruct((n, LANES), F32),
            jax.ShapeDtypeStruct((nt, 4 * TOP_K, t), F32),
            jax.ShapeDtypeStruct((nt, SUBLANES, LANES), F32),
        ],
        compiler_params=_params(("parallel",)),
        name="outproj_router",
    )(y_attn, y_rwkv, x2, w_out_bf16, gain.reshape(1, d), rwt, rbt)


def _run_layout(plen_f):
    plen = plen_f[:, 0, :N_EXPERTS].astype(I32)
    tot = jnp.sum(plen, axis=0)
    reg = (tot + EXPERT_BLOCK - 1) // EXPERT_BLOCK * EXPERT_BLOCK
    reg_end = jnp.cumsum(reg)
    estart = reg_end - reg
    goff = estart[None, :] + jnp.cumsum(plen, axis=0) - plen
    return plen, goff, estart + tot, reg - tot, reg_end


TAIL_BITS = tuple(1 << b for b in range(EXPERT_BLOCK.bit_length() - 2, RUN_ALIGN.bit_length() - 2, -1))
assert TAIL_BITS[0] == EXPERT_BLOCK // 2 and TAIL_BITS[-1] == RUN_ALIGN


def _start_run_copies(plen_ref, goff_ref, tile, local_ref, hbm_ref, sem, to_hbm):
    def body(e, lo):
        n = plen_ref[tile * N_EXPERTS + e]
        g = goff_ref[tile * N_EXPERTS + e]

        @pl.when(n > 0)
        def _():
            rows = pl.multiple_of(n, RUN_ALIGN)
            loc = local_ref.at[pl.ds(pl.multiple_of(lo, RUN_ALIGN), rows), :]
            glob = hbm_ref.at[pl.ds(pl.multiple_of(g, RUN_ALIGN), rows), :]
            if to_hbm:
                pltpu.make_async_copy(loc, glob, sem).start()
            else:
                pltpu.make_async_copy(glob, loc, sem).start()
        return lo + n
    lax.fori_loop(0, N_EXPERTS, body, 0)


def _wait_rows(total, local_ref, hbm_ref, sem, to_hbm):
    @pl.when(total > 0)
    def _():
        rows = pl.multiple_of(total, RUN_ALIGN)
        loc = local_ref.at[pl.ds(0, rows), :]
        glob = hbm_ref.at[pl.ds(0, rows), :]
        if to_hbm:
            pltpu.make_async_copy(loc, glob, sem).wait()
        else:
            pltpu.make_async_copy(glob, loc, sem).wait()


def _dispatch_kernel(plen_ref, goff_ref, ttot_ref, tstart_ref, tlen_ref, h_ref, routet_ref, xs_ref,
                     buf_ref, zero_ref, sem_ref):
    t = pl.program_id(0)
    nt = pl.num_programs(0)
    slot = t & 1
    other = 1 - slot
    tt = h_ref.shape[0]

    pos_t = routet_ref[2 * TOP_K:3 * TOP_K, :].astype(I32)
    rows = lax.broadcasted_iota(I32, (TILE_ROWS, tt), 0)
    hit = jnp.zeros((TILE_ROWS, tt), F32)
    for k in range(TOP_K):
        hit = jnp.where(rows == pos_t[k:k + 1, :], 1.0, hit)
    buf_ref[slot] = _dot(hit.astype(BF16), h_ref[...])
    _start_run_copies(plen_ref, goff_ref, t, buf_ref.at[slot], xs_ref, sem_ref.at[slot], True)

    @pl.when(t > 0)
    def _():
        _wait_rows(ttot_ref[t - 1], buf_ref.at[other], xs_ref, sem_ref.at[other], True)

    @pl.when(t == nt - 1)
    def _():
        _wait_rows(ttot_ref[t], buf_ref.at[slot], xs_ref, sem_ref.at[slot], True)
        zero_ref[...] = jnp.zeros_like(zero_ref)
        zrows = zero_ref.shape[0]
        for wait in (False, True):
            def body(e, c):
                n = tlen_ref[e]
                g = tstart_ref[e]
                for bit in TAIL_BITS:
                    @pl.when((n & bit) != 0)
                    def _():
                        o = n & (-2 * bit)
                        cp = pltpu.make_async_copy(
                            zero_ref.at[pl.ds(0, bit), :],
                            xs_ref.at[pl.ds(pl.multiple_of(g + o, RUN_ALIGN), bit), :], sem_ref.at[2])
                        if wait:
                            cp.wait()
                        else:
                            cp.start()
                return c
            lax.fori_loop(0, N_EXPERTS, body, 0)

            def slack(i, c):
                g = tstart_ref[N_EXPERTS] + i * zrows
                cp = pltpu.make_async_copy(
                    zero_ref, xs_ref.at[pl.ds(pl.multiple_of(g, RUN_ALIGN), zrows), :], sem_ref.at[2])
                if wait:
                    cp.wait()
                else:
                    cp.start()
                return c
            lax.fori_loop(0, tlen_ref[N_EXPERTS] // zrows, slack, 0)


def _dispatch(h_bf16, route_t, plen, goff, tstart, tlen, total_rows):
    n, d = h_bf16.shape
    t = ROUTE_TILE
    grid_spec = pltpu.PrefetchScalarGridSpec(
        num_scalar_prefetch=5,
        grid=(n // t,),
        in_specs=[
            pl.BlockSpec((t, d), lambda i, *_: (i, 0)),
            pl.BlockSpec((None, 4 * TOP_K, t), lambda i, *_: (i, 0, 0)),
        ],
        out_specs=pl.BlockSpec(memory_space=pl.ANY),
        scratch_shapes=[
            pltpu.VMEM((2, TILE_ROWS, d), F32),
            pltpu.VMEM((EXPERT_BLOCK // 2, d), F32),
            pltpu.SemaphoreType.DMA((3,)),
        ],
    )
    return pl.pallas_call(
        _dispatch_kernel,
        grid_spec=grid_spec,
        out_shape=jax.ShapeDtypeStruct((total_rows, d), F32),
        compiler_params=_params(("arbitrary",)),
        name="moe_dispatch",
    )(plen.reshape(-1), goff.reshape(-1), jnp.sum(plen, axis=1), tstart, tlen, h_bf16, route_t)


def _experts_kernel(bexp_ref, nvalid_ref, x_ref, w1_ref, b1_ref, w2_ref, b2_ref, y_ref,
                    w1b_ref, w2b_ref):
    j = pl.program_id(0)
    valid = j < nvalid_ref[0]
    changed = (j == 0) | (bexp_ref[j] != bexp_ref[jnp.maximum(j - 1, 0)])
    f = w2_ref.shape[0]

    @pl.when(valid & changed)
    def _():
        w1b_ref[...] = w1_ref[...].astype(BF16)
        w2b_ref[...] = w2_ref[...].astype(BF16)

    @pl.when(valid)
    def _():
        rows = x_ref.shape[0] // EXPERT_SPLIT
        parts = [slice(i * rows, (i + 1) * rows) for i in range(EXPERT_SPLIT)]
        us = [_dot(x_ref[r, :].astype(BF16), w1b_ref[...]) + b1_ref[...] for r in parts]
        acts = []
        for u in us:
            glu = jnp.minimum(u[:, :f], SWIGLU_LIMIT)
            lin = jnp.clip(u[:, f:], -SWIGLU_LIMIT, SWIGLU_LIMIT)
            acts.append((glu * _sigmoid(SWIGLU_ALPHA * glu) * (lin + 1.0)).astype(BF16))
        for r, act in zip(parts, acts):
            y_ref[r, :] = _dot(act, w2b_ref[...]) + b2_ref[...]

    @pl.when(jnp.logical_not(valid))
    def _():
        y_ref[...] = jnp.zeros_like(y_ref)


def _experts(xs, block_expert, nvalid, w1, b1, w2, b2):
    rows, d = xs.shape
    bm = EXPERT_BLOCK
    f2 = w1.shape[2]
    f = w2.shape[1]
    grid_spec = pltpu.PrefetchScalarGridSpec(
        num_scalar_prefetch=2,
        grid=(rows // bm,),
        in_specs=[
            pl.BlockSpec((bm, d), lambda j, be, nv: (jnp.minimum(j, jnp.maximum(nv[0] - 1, 0)), 0)),
            pl.BlockSpec((None, d, f2), lambda j, be, nv: (be[j], 0, 0)),
            pl.BlockSpec((None, 1, f2), lambda j, be, nv: (be[j], 0, 0)),
            pl.BlockSpec((None, f, d), lambda j, be, nv: (be[j], 0, 0)),
            pl.BlockSpec((None, 1, d), lambda j, be, nv: (be[j], 0, 0)),
        ],
        out_specs=pl.BlockSpec((bm, d), lambda j, be, nv: (j, 0)),
        scratch_shapes=[pltpu.VMEM((d, f2), BF16), pltpu.VMEM((f, d), BF16)],
    )
    e = w1.shape[0]
    return pl.pallas_call(
        _experts_kernel,
        grid_spec=grid_spec,
        out_shape=jax.ShapeDtypeStruct((rows, d), F32),
        compiler_params=_params(("arbitrary",)),
        name="moe_experts",
    )(block_expert, nvalid, xs, w1, b1.reshape(e, 1, f2), w2, b2.reshape(e, 1, d))


def _combine_kernel(plen_ref, goff_ref, ttot_ref, route_ref, x1_ref, g_ref, ys_ref, o_ref,
                    buf_ref, sem_ref):
    t = pl.program_id(0)
    nt = pl.num_programs(0)
    slot = t & 1
    other = 1 - slot

    @pl.when(t == 0)
    def _():
        buf_ref[...] = jnp.zeros_like(buf_ref)
        _start_run_copies(plen_ref, goff_ref, t, buf_ref.at[0], ys_ref, sem_ref.at[0], False)

    @pl.when(t + 1 < nt)
    def _():
        _start_run_copies(plen_ref, goff_ref, t + 1, buf_ref.at[other], ys_ref, sem_ref.at[other], False)

    route = route_ref[...]
    tt = route.shape[0]
    col = lax.broadcasted_iota(I32, (tt, TILE_ROWS), 1)
    wts = jnp.zeros((tt, TILE_ROWS), F32)
    for k in range(TOP_K):
        pos_k = route[:, 2 * TOP_K + k:2 * TOP_K + k + 1].astype(I32)
        wts = jnp.where(col == pos_k, route[:, TOP_K + k:TOP_K + k + 1], wts)

    _wait_rows(ttot_ref[t], buf_ref.at[slot], ys_ref, sem_ref.at[slot], False)
    x2 = x1_ref[...] + _dot(wts.astype(BF16), buf_ref[slot].astype(BF16))
    ms = jnp.mean(x2 * x2, axis=-1, keepdims=True)
    o_ref[...] = x2 * lax.rsqrt(ms + NORM_EPS) * g_ref[...]


def _combine(ys, route, x1, gain, plen, goff):
    n, d = x1.shape
    t = ROUTE_TILE
    grid_spec = pltpu.PrefetchScalarGridSpec(
        num_scalar_prefetch=3,
        grid=(n // t,),
        in_specs=[
            pl.BlockSpec((t, LANES), lambda i, *_: (i, 0)),
            pl.BlockSpec((t, d), lambda i, *_: (i, 0)),
            pl.BlockSpec((1, d), lambda i, *_: (0, 0)),
            pl.BlockSpec(memory_space=pl.ANY),
        ],
        out_specs=pl.BlockSpec((t, d), lambda i, *_: (i, 0)),
        scratch_shapes=[
            pltpu.VMEM((2, TILE_ROWS, d), F32),
            pltpu.SemaphoreType.DMA((2,)),
        ],
    )
    return pl.pallas_call(
        _combine_kernel,
        grid_spec=grid_spec,
        out_shape=jax.ShapeDtypeStruct((n, d), F32),
        compiler_params=_params(("arbitrary",)),
        name="moe_combine",
    )(plen.reshape(-1), goff.reshape(-1), jnp.sum(plen, axis=1), route, x1, gain.reshape(1, d), ys)


def _moe_rows(n):
    nt = n // ROUTE_TILE
    worst = TOP_K * n + nt * N_EXPERTS * (RUN_ALIGN - 1) + N_EXPERTS * (EXPERT_BLOCK - RUN_ALIGN)
    return -(-worst // EXPERT_BLOCK) * EXPERT_BLOCK


def _moe_and_final_norm(h_bf16, route, route_t, plen_f, x1, w1, b1, w2, b2, final_gain):
    n = x1.shape[0]
    rows = _moe_rows(n)
    plen, goff, tstart, tlen, reg_end = _run_layout(plen_f)
    nblocks = rows // EXPERT_BLOCK
    starts = jnp.arange(nblocks, dtype=I32) * EXPERT_BLOCK
    nvalid = (reg_end[-1] // EXPERT_BLOCK).astype(I32).reshape(1)
    bexp = jnp.sum((reg_end[None, :] <= starts[:, None]).astype(I32), axis=1)
    last = jnp.take(bexp, jnp.maximum(nvalid[0] - 1, 0))
    bexp = jnp.where(starts < reg_end[-1], bexp, last)
    tstart = jnp.concatenate([tstart, reg_end[-1:]]).astype(I32)
    tlen = jnp.concatenate([tlen, rows - reg_end[-1:]]).astype(I32)
    xs = _dispatch(h_bf16, route_t, plen, goff, tstart, tlen, rows)
    ys = _experts(xs, bexp, nvalid, w1, b1, w2, b2)
    return _combine(ys, route, x1, final_gain, plen, goff)


def kernel(x, norm1_gain, w_in, shift_mu, decay_up, decay_bias, iclr_up, iclr_bias, gate_up, k_k, k_a, r_k, lnx_w, lnx_b, attn_out_gain, w_out, norm2_gain, router_w, router_b, expert_w1, expert_b1, expert_w2, expert_b2, final_norm_gain):
    b, s, d = x.shape
    x2 = x.reshape(b * s, d)
    z_attn, z_rwkv = _inproj(x2, norm1_gain[0], w_in[0].astype(BF16))
    y_attn = _attention(z_attn, attn_out_gain[0], b, s)
    y_rwkv = _rwkv(z_rwkv, b, s, shift_mu[0], decay_up[0], decay_bias[0], iclr_up[0], iclr_bias[0],
                   gate_up[0], k_k[0], k_a[0], r_k[0], lnx_w[0], lnx_b[0])
    x1, h2, route, route_t, plen_f = _outproj_router(
        y_attn, y_rwkv, x2, w_out[0].astype(BF16), norm2_gain[0], router_w[0], router_b[0])
    out = _moe_and_final_norm(h2, route, route_t, plen_f, x1, expert_w1[0], expert_b1[0],
                              expert_w2[0], expert_b2[0], final_norm_gain)
    return out.reshape(b, s, d)
```

```python
import jax
import jax.numpy as jnp
from jax import lax
from jax.experimental import pallas as pl
from jax.experimental.pallas import tpu as pltpu

F32 = jnp.float32
BF16 = jnp.bfloat16
I32 = jnp.int32

D_MODEL = 1024
HEADS = 8
HEAD_DIM = 64
ATTN_WIDTH = HEADS * HEAD_DIM
RWKV_WIDTH = HEADS * HEAD_DIM
DECAY_LORA = 64
ICLR_LORA = 64
GATE_LORA = 128
RWKV_IN = 3 * RWKV_WIDTH + DECAY_LORA + ICLR_LORA + GATE_LORA
IN_WIDTH = 3 * ATTN_WIDTH + RWKV_IN
N_EXPERTS = 32
TOP_K = 4
SWIGLU_ALPHA = 1.702
SWIGLU_LIMIT = 7.0
NORM_EPS = 1e-5
GROUPNORM_EPS = 64e-5
L2_EPS = 1e-12

LANES = 128
SUBLANES = 8
VMEM_LIMIT = 56 * 1024 * 1024
NEG_INF = float("-inf")


def _dot(a, b):
    return jnp.dot(a, b, preferred_element_type=F32)


def _dot_nt(a, b):
    return lax.dot_general(a, b, (((1,), (1,)), ((), ())), preferred_element_type=F32)


def _params(sem, vmem=VMEM_LIMIT):
    return pltpu.CompilerParams(dimension_semantics=sem, vmem_limit_bytes=vmem)


def _inproj_kernel(x_ref, g_ref, w_ref, za_ref, zr_ref):
    x = x_ref[...]
    ms = jnp.mean(x * x, axis=-1, keepdims=True)
    h = (x * lax.rsqrt(ms + NORM_EPS) * g_ref[...]).astype(BF16)
    na = za_ref.shape[1]
    za_ref[...] = _dot(h, w_ref[:, :na])
    zr_ref[...] = _dot(h, w_ref[:, na:])


def _inproj(x2, gain, w_in_bf16, tm=512):
    n, d = x2.shape
    na = 3 * ATTN_WIDTH
    return pl.pallas_call(
        _inproj_kernel,
        grid=(n // tm,),
        in_specs=[
            pl.BlockSpec((tm, d), lambda i: (i, 0)),
            pl.BlockSpec((1, d), lambda i: (0, 0)),
            pl.BlockSpec((d, IN_WIDTH), lambda i: (0, 0)),
        ],
        out_specs=[
            pl.BlockSpec((tm, na), lambda i: (i, 0)),
            pl.BlockSpec((tm, RWKV_IN), lambda i: (i, 0)),
        ],
        out_shape=[
            jax.ShapeDtypeStruct((n, na), F32),
            jax.ShapeDtypeStruct((n, RWKV_IN), F32),
        ],
        compiler_params=_params(("parallel",)),
        name="inproj",
    )(x2, gain.reshape(1, d), w_in_bf16)


ATT_BLK = 128
ATT_DIL = 16
ATT_BLOCKS_PER_ITER = 16
ATT_CLASSES_PER_ITER = 4
ATT_CLASSES_PER_GROUP = 4
LOG2_E = 1.4426950408889634


def _attn_kernel(q_ref, k_ref, v_ref, g_ref, o_ref,
                 kp_ref, vp_ref, m1_ref, l1_ref, a1_ref, of_ref):
    seq = q_ref.shape[0]
    nblk = seq // ATT_BLK
    scale = HEAD_DIM ** -0.5 * LOG2_E
    lane = lax.broadcasted_iota(I32, (1, LANES), 1)
    head_masks = (lane < HEAD_DIM, lane >= HEAD_DIM)

    for r in range(ATT_DIL):
        pb = ((r % 4) * 4 + r // 4) * ATT_BLK
        kp_ref[pb:pb + ATT_BLK, :] = k_ref[pl.ds(r, ATT_BLK, stride=ATT_DIL), :].astype(BF16)
        vp_ref[pb:pb + ATT_BLK, :] = v_ref[pl.ds(r, ATT_BLK, stride=ATT_DIL), :].astype(BF16)

    ii = lax.broadcasted_iota(I32, (ATT_BLK, ATT_BLK), 0)
    jj = lax.broadcasted_iota(I32, (ATT_BLK, ATT_BLK), 1)

    def pack_heads(vals):
        return jnp.where(head_masks[0], vals[0], vals[1])

    cur_ok = jj <= ii
    prev_band = jj >= ii

    def blk_group(g, carry):
        base = g * ATT_BLOCKS_PER_ITER
        rows = [pl.ds(pl.multiple_of((base + u) * ATT_BLK, ATT_BLK), ATT_BLK)
                for u in range(ATT_BLOCKS_PER_ITER)]
        p0 = pl.ds(pl.multiple_of(jnp.maximum(base - 1, 0) * ATT_BLK, ATT_BLK), ATT_BLK)
        qs = [q_ref[rw, :] * scale for rw in rows]
        ks = [k_ref[p0, :].astype(BF16)] + [k_ref[rw, :].astype(BF16) for rw in rows]
        vs = [v_ref[p0, :].astype(BF16)] + [v_ref[rw, :].astype(BF16) for rw in rows]
        prev_ok = [prev_band & (g > 0)] + [prev_band] * (ATT_BLOCKS_PER_ITER - 1)
        chains = [(u, h) for u in range(ATT_BLOCKS_PER_ITER) for h in range(2)]
        qh = [jnp.where(head_masks[h], qs[u], 0.0).astype(BF16) for u, h in chains]
        sc = [jnp.where(cur_ok, _dot_nt(q, ks[u + 1]), NEG_INF) for q, (u, h) in zip(qh, chains)]
        sp = [jnp.where(prev_ok[u], _dot_nt(q, ks[u]), NEG_INF) for q, (u, h) in zip(qh, chains)]
        m = [jnp.max(jnp.maximum(a, b), axis=-1, keepdims=True) for a, b in zip(sc, sp)]
        pc = [jnp.exp2(a - mm) for a, mm in zip(sc, m)]
        pp = [jnp.exp2(b - mm) for b, mm in zip(sp, m)]
        ls = [jnp.sum(a + b, axis=-1, keepdims=True) for a, b in zip(pc, pp)]
        acc = [_dot(a.astype(BF16), vs[u + 1]) + _dot(b.astype(BF16), vs[u])
               for a, b, (u, h) in zip(pc, pp, chains)]
        for u, rw in enumerate(rows):
            m1_ref[rw, :] = pack_heads(m[2 * u:2 * u + 2])
            l1_ref[rw, :] = pack_heads(ls[2 * u:2 * u + 2])
            a1_ref[rw, :] = pack_heads(acc[2 * u:2 * u + 2])
        return carry

    lax.fori_loop(0, nblk // ATT_BLOCKS_PER_ITER, blk_group, 0)

    nk = 4 * ATT_BLK
    qi = lax.broadcasted_iota(I32, (ATT_BLK, nk), 0)
    kc_i = lax.broadcasted_iota(I32, (ATT_BLK, nk), 1)
    k_hi = kc_i >> 7
    k_pos = kc_i & (ATT_BLK - 1)
    gain = g_ref[...]

    def res_group(g, carry):
        per_hi = 4 // ATT_CLASSES_PER_ITER
        r_hi = g // per_hi
        rs = [4 * r_hi + (g % per_hi) * ATT_CLASSES_PER_ITER + u for u in range(ATT_CLASSES_PER_ITER)]
        rows = [pl.ds(r, ATT_BLK, stride=ATT_DIL) for r in rs]
        qr = [q_ref[rw, :] * scale for rw in rows]
        k0 = [pl.ds(pl.multiple_of((r & 3) * nk, nk), nk) for r in rs]
        kc = [kp_ref[kk, :] for kk in k0]
        vc = [vp_ref[kk, :] for kk in k0]
        delta = ATT_DIL * (qi - k_pos) + 4 * (r_hi - k_hi)
        in4 = (delta >= 0) & (delta <= 4 * ATT_BLK)
        in16 = (k_hi == r_hi) & (delta >= 0)
        bias = jnp.where(in4 & in16, 1.0, jnp.where(in4 | in16, 0.0, NEG_INF))
        for u0 in range(0, ATT_CLASSES_PER_ITER, ATT_CLASSES_PER_GROUP):
            us = range(u0, u0 + ATT_CLASSES_PER_GROUP)
            chains = [(u, h) for u in us for h in range(2)]
            qh = [jnp.where(head_masks[h], qr[u], 0.0).astype(BF16) for u, h in chains]
            s = [_dot_nt(q, kc[u]) + bias for q, (u, h) in zip(qh, chains)]
            m = [jnp.max(a, axis=-1, keepdims=True) for a in s]
            e = [jnp.exp2(a - mm) for a, mm in zip(s, m)]
            ls = [jnp.sum(a, axis=-1, keepdims=True) for a in e]
            acc = [_dot(a.astype(BF16), vc[u]) for a, (u, h) in zip(e, chains)]
            for j, u in enumerate(us):
                rw = rows[u]
                m2, l2, a2 = (pack_heads(x[2 * j:2 * j + 2]) for x in (m, ls, acc))
                m1 = m1_ref[rw, :]
                mx = jnp.maximum(m1, m2)
                w1 = jnp.exp2(m1 - mx)
                w2 = jnp.exp2(m2 - mx)
                den = l1_ref[rw, :] * w1 + l2 * w2
                o = (a1_ref[rw, :] * w1 + a2 * w2) / den
                o2 = o * o
                msq = pack_heads([jnp.sum(jnp.where(hm, o2, 0.0), axis=-1, keepdims=True)
                                  for hm in head_masks])
                of_ref[rw, :] = o * lax.rsqrt(msq * (1.0 / HEAD_DIM) + NORM_EPS) * gain
        return carry

    lax.fori_loop(0, ATT_DIL // ATT_CLASSES_PER_ITER, res_group, 0)
    o_ref[...] = of_ref[...].astype(o_ref.dtype)


def _attention(z_attn, out_gain, batch, seq):
    assert seq == ATT_BLK * ATT_DIL
    pairs = ATTN_WIDTH // LANES
    z3 = z_attn.reshape(batch, seq, 3 * ATTN_WIDTH)
    blk = lambda off: pl.BlockSpec((None, seq, LANES), lambda b, p: (b, 0, off + p))
    out = pl.pallas_call(
        _attn_kernel,
        grid=(batch, pairs),
        in_specs=[blk(0), blk(pairs), blk(2 * pairs),
                  pl.BlockSpec((1, LANES), lambda b, p: (0, p))],
        out_specs=pl.BlockSpec((None, seq, LANES), lambda b, p: (b, 0, p)),
        out_shape=jax.ShapeDtypeStruct((batch, seq, ATTN_WIDTH), BF16),
        scratch_shapes=[
            pltpu.VMEM((seq, LANES), BF16), pltpu.VMEM((seq, LANES), BF16),
            pltpu.VMEM((seq, LANES), F32), pltpu.VMEM((seq, LANES), F32),
            pltpu.VMEM((seq, LANES), F32), pltpu.VMEM((seq, LANES), F32),
        ],
        compiler_params=_params(("parallel", "parallel")),
        name="dilated_attn",
    )(z3, z3, z3, out_gain.reshape(1, ATTN_WIDTH))
    return out.reshape(batch * seq, ATTN_WIDTH)


WKV_CHUNK = 64
WKV_SEQS_PER_STEP = 8
WKV_SEQS_PER_GROUP = 4


def _split3(x):
    h = x.astype(BF16)
    r1 = x - h.astype(F32)
    m = r1.astype(BF16)
    l = (r1 - m.astype(F32)).astype(BF16)
    return h, m, l


def _seg_sum(x, seg):
    xb = x.astype(BF16)
    return jnp.concatenate(
        [_dot(xb[:, p * LANES:(p + 1) * LANES], seg) for p in range(x.shape[1] // LANES)], axis=1)


def _softplus(y):
    return jnp.maximum(y, 0.0) + jnp.log(1.0 + jnp.exp(-jnp.abs(y)))


def _sigmoid(y):
    return 1.0 / (1.0 + jnp.exp(-y))


def _rwkv_kernel(z_ref, mu_ref, wl_ref, db_ref, ib_ref, gu_ref, kk_ref, ka_ref, rk_ref,
                 lnw_ref, lnb_ref, o_ref, st_ref, prev_ref):
    nseq, cs, _ = z_ref.shape
    w = RWKV_WIDTH
    pairs = w // LANES
    n2 = 2 * cs
    assert n2 == LANES

    @pl.when(pl.program_id(1) == 0)
    def _():
        st_ref[...] = jnp.zeros_like(st_ref)
        prev_ref[...] = jnp.zeros_like(prev_ref)

    lane = lax.broadcasted_iota(I32, (1, LANES), 1)
    head_a = lane < HEAD_DIM
    li = lax.broadcasted_iota(I32, (LANES, LANES), 0)
    lj = lax.broadcasted_iota(I32, (LANES, LANES), 1)
    seg = ((li >> 6) == (lj >> 6)).astype(BF16)
    ti = lax.broadcasted_iota(I32, (cs, cs), 0)
    tj = lax.broadcasted_iota(I32, (cs, cs), 1)
    tri = (ti >= tj).astype(BF16)
    strict = li < lj
    incl = li <= lj
    eye = (li == lj).astype(F32)
    row = lax.broadcasted_iota(I32, (cs, 1), 0)

    def stack2(x, dtype=BF16):
        return jnp.concatenate([jnp.where(head_a, x, 0.0), jnp.where(head_a, 0.0, x)], axis=0).astype(dtype)

    seqs = {}
    sl = lambda p: slice(p * LANES, (p + 1) * LANES)
    inv_n = 1.0 / HEAD_DIM

    def prepare(i):
        z = z_ref[i]
        zprev = jnp.where(row == 0, prev_ref[i, 0:1, :], pltpu.roll(z, 1, axis=0))
        prev_ref[i, 0:1, :] = z[cs - 1:cs, :]
        zs = z + (zprev - z) * mu_ref[...]
        r_ = zs[:, 0:w]
        k_ = zs[:, w:2 * w]
        v_ = zs[:, 2 * w:3 * w]
        lora_in = zs[:, 3 * w:3 * w + LANES]
        gate_in = zs[:, 3 * w + LANES:]
        yield
        xl = jnp.where(head_a, jnp.tanh(lora_in), lora_in).astype(BF16)
        pre = _dot(xl, wl_ref[...])
        wlog = -_softplus(-(pre[:, :w] + db_ref[...])) - 0.5
        lw = -jnp.exp(wlog)
        yield
        rate = _sigmoid(pre[:, w:] + ib_ref[...])
        gate = _dot(_sigmoid(gate_in).astype(BF16), gu_ref[...])
        yield
        kk = k_ * kk_ref[...]
        kk = kk / jnp.maximum(jnp.sqrt(_seg_sum(kk * kk, seg)), L2_EPS)
        kp = k_ * (1.0 + (rate - 1.0) * ka_ref[...])
        bn = kk * rate
        yield
        cum = sum(_dot(tri, t) for t in _split3(lw))
        cend = cum[cs - 1:cs, :]
        yield
        e_inv = jnp.exp(-cum)
        e_end = jnp.exp(cend - cum)
        at = -kk * jnp.exp(cum - lw)
        rt = r_ * jnp.exp(cum)
        yield
        seqs[i] = dict(r=r_, v=v_, kp=kp, gate=gate, g_end=jnp.exp(cend), at=at, rt=rt,
                       bt=bn * e_inv, kt=kp * e_inv, bp=bn * e_end, kpe=kp * e_end)

    def finish(i, y):
        q = seqs[i]
        mean = _seg_sum(y, seg) * inv_n
        yield
        yc = y - mean
        var = _seg_sum(yc * yc, seg) * inv_n
        yield
        y = yc * lax.rsqrt(var + GROUPNORM_EPS) * lnw_ref[...] + lnb_ref[...]
        y = y + _seg_sum(q["r"] * q["kp"] * rk_ref[...], seg) * q["v"]
        yield
        o_ref[i] = (y * q["gate"]).astype(o_ref.dtype)

    side = []

    def tick():
        for gen in list(side):
            if next(gen, side) is side:
                side.remove(gen)

    def drain():
        while side:
            tick()

    def run_chains(group):
        chains = [(i, p) for i in group for p in range(pairs)]
        ar = [jnp.concatenate([stack2(seqs[i]["at"][:, sl(p)]), stack2(seqs[i]["rt"][:, sl(p)])], axis=0)
              for i, p in chains]
        bk = [jnp.concatenate([stack2(seqs[i]["bt"][:, sl(p)]), stack2(seqs[i]["kt"][:, sl(p)])], axis=0)
              for i, p in chains]
        g1 = [_dot_nt(b, a) for b, a in zip(bk, ar)]
        tick()
        ab_t = [jnp.where(strict, g[:n2, :n2], 0.0) for g in g1]
        rb_t = [jnp.where(incl, g[:n2, n2:], 0.0).astype(BF16) for g in g1]
        akrk = [jnp.concatenate([jnp.where(strict, g[n2:, :n2], 0.0), jnp.where(incl, g[n2:, n2:], 0.0)],
                                axis=1).astype(BF16) for g in g1]

        tt = [eye + jnp.where((li >> 1) == (lj >> 1), a, 0.0) for a in ab_t]
        s = 2
        while s < cs:
            sh = s.bit_length() - 1
            off = ((li >> (sh + 1)) == (lj >> (sh + 1))) & ((li >> sh) != (lj >> sh))
            tb = [t.astype(BF16) for t in tt]
            ta = [_dot(t, jnp.where(off, a, 0.0).astype(BF16)).astype(BF16) for t, a in zip(tb, ab_t)]
            tick()
            tt = [t + _dot(x, b) for t, x, b in zip(tt, ta, tb)]
            tick()
            s *= 2
        tt = [t.astype(BF16) for t in tt]

        v_t = [stack2(seqs[i]["v"][:, sl(p)], F32).T.astype(BF16) for i, p in chains]
        s0 = [st_ref[i, p] for i, p in chains]
        g2 = [_dot_nt(s.astype(BF16), a) for s, a in zip(s0, ar)]
        tick()
        g3 = [_dot(v, m) for v, m in zip(v_t, akrk)]
        tick()
        u_t = [_dot((a[:, :n2] + b[:, :n2]).astype(BF16), t).astype(BF16) for a, b, t in zip(g2, g3, tt)]
        tick()
        y_t = [a[:, n2:] + b[:, n2:] + _dot(u, m) for a, b, u, m in zip(g2, g3, u_t, rb_t)]
        tick()
        for c, (i, p) in enumerate(chains):
            bkp = jnp.concatenate([stack2(seqs[i]["bp"][:, sl(p)]), stack2(seqs[i]["kpe"][:, sl(p)])], axis=0)
            st_ref[i, p] = s0[c] * seqs[i]["g_end"][:, sl(p)] + _dot(
                jnp.concatenate([u_t[c], v_t[c]], axis=1), bkp)
        ys = [jnp.where(head_a, yt[:cs, :], yt[cs:, :]) for yt in (y.T for y in y_t)]
        return [jnp.concatenate(ys[j * pairs:(j + 1) * pairs], axis=1) for j in range(len(group))]

    groups = [range(g0, min(g0 + WKV_SEQS_PER_GROUP, nseq)) for g0 in range(0, nseq, WKV_SEQS_PER_GROUP)]
    side.extend(prepare(i) for i in groups[0])
    drain()
    for gi, group in enumerate(groups):
        if gi + 1 < len(groups):
            side.extend(prepare(i) for i in groups[gi + 1])
        outs = run_chains(group)
        drain()
        side.extend(finish(i, y) for i, y in zip(group, outs))
    drain()


def _rwkv(z_rwkv, batch, seq, shift_mu, decay_up, decay_bias, iclr_up, iclr_bias, gate_up,
          k_k, k_a, r_k, lnx_w, lnx_b):
    cs = WKV_CHUNK
    w = RWKV_WIDTH
    nb = WKV_SEQS_PER_STEP if batch % WKV_SEQS_PER_STEP == 0 else 1
    z3 = z_rwkv.reshape(batch, seq, RWKV_IN)
    wl = jnp.zeros((LANES, 2 * w), F32)
    wl = wl.at[:DECAY_LORA, :w].set(decay_up).at[DECAY_LORA:, w:].set(iclr_up).astype(BF16)
    row = lambda a: a.reshape(1, -1).astype(F32)
    full = lambda shape: pl.BlockSpec(shape, lambda b, c: (0,) * len(shape))
    out = pl.pallas_call(
        _rwkv_kernel,
        grid=(batch // nb, seq // cs),
        in_specs=[
            pl.BlockSpec((nb, cs, RWKV_IN), lambda b, c: (b, c, 0)),
            full((1, RWKV_IN)), full((LANES, 2 * w)), full((1, w)), full((1, w)),
            full((GATE_LORA, w)), full((1, w)), full((1, w)), full((1, w)), full((1, w)), full((1, w)),
        ],
        out_specs=pl.BlockSpec((nb, cs, w), lambda b, c: (b, c, 0)),
        out_shape=jax.ShapeDtypeStruct((batch, seq, w), BF16),
        scratch_shapes=[
            pltpu.VMEM((nb, w // LANES, LANES, LANES), F32),
            pltpu.VMEM((nb, SUBLANES, RWKV_IN), F32),
        ],
        compiler_params=_params(("parallel", "arbitrary")),
        name="rwkv7",
    )(z3, row(shift_mu), wl, row(decay_bias), row(iclr_bias), gate_up.astype(BF16),
      row(k_k), row(k_a), row(r_k), row(lnx_w), row(lnx_b))
    return out.reshape(batch * seq, w)


ROUTE_TILE = 256
RUN_ALIGN = SUBLANES
TILE_ROWS = 1280
EXPERT_BLOCK = 512
EXPERT_SPLIT = 2
ROUTER_TILES_PER_STEP = 4
assert TILE_ROWS >= TOP_K * ROUTE_TILE + N_EXPERTS * (RUN_ALIGN - 1) and TILE_ROWS % LANES == 0
HIGHEST = lax.Precision.HIGHEST


def _router_kernel(ya_ref, yr_ref, x_ref, wo_ref, g_ref, rwt_ref, rbt_ref,
                   x1_ref, h_ref, route_ref, routet_ref, plen_ref):
    t = ROUTE_TILE
    ne = N_EXPERTS
    half = ya_ref.shape[1]
    tiles = range(x_ref.shape[0] // t)
    rows = [slice(s * t, (s + 1) * t) for s in tiles]
    x1 = [x_ref[r, :] + _dot(ya_ref[r, :], wo_ref[:half, :]) + _dot(yr_ref[r, :], wo_ref[half:, :])
          for r in rows]
    h = []
    for r, v in zip(rows, x1):
        x1_ref[r, :] = v
        ms = jnp.mean(v * v, axis=-1, keepdims=True)
        h.append(v * lax.rsqrt(ms + NORM_EPS) * g_ref[...])
    h_hi = [v.astype(BF16) for v in h]
    for r, v in zip(rows, h_hi):
        h_ref[r, :] = v

    h_lo = [(v - vb.astype(F32)).astype(BF16) for v, vb in zip(h, h_hi)]
    both = [_dot_nt(rwt_ref[...], vb) for vb in h_hi]
    lg = [bt[:ne, :] + bt[ne:, :] + _dot_nt(rwt_ref[:ne, :], vl) + rbt_ref[...]
          for bt, vl in zip(both, h_lo)]
    e_f = lax.broadcasted_iota(I32, (ne, t), 0).astype(F32)
    vals = [[] for _ in tiles]
    hots = [[] for _ in tiles]
    for _ in range(TOP_K):
        for s in tiles:
            m = jnp.max(lg[s], axis=0, keepdims=True)
            idx = jnp.min(jnp.where(lg[s] == m, e_f, float(ne)), axis=0, keepdims=True)
            hot = e_f == idx
            vals[s].append(m)
            hots[s].append(hot)
            lg[s] = jnp.where(hot, NEG_INF, lg[s])

    ti = lax.broadcasted_iota(I32, (t, t), 0)
    tj = lax.broadcasted_iota(I32, (t, t), 1)
    earlier = (ti < tj).astype(BF16)
    ei = lax.broadcasted_iota(I32, (ne, ne), 0)
    ej = lax.broadcasted_iota(I32, (ne, ne), 1)
    lower = (ej < ei).astype(F32)
    multi = [sum(hh.astype(F32) for hh in hots[s]) for s in tiles]
    before = [_dot(mm.astype(BF16), earlier) for mm in multi]
    padded = [jnp.ceil(jnp.sum(mm, axis=1, keepdims=True) * (1.0 / RUN_ALIGN)) * RUN_ALIGN
              for mm in multi]
    run_start = [jnp.dot(lower, jnp.broadcast_to(pd, (ne, LANES)), precision=HIGHEST,
                         preferred_element_type=F32)[:, 0:1] for pd in padded]
    ones = jnp.ones((2 * SUBLANES, t), BF16)
    counts_row = [_dot_nt(ones, jnp.concatenate([mm, jnp.zeros((LANES - ne, t), F32)], axis=0).astype(BF16))
                  for mm in multi]

    blank = jnp.zeros((TOP_K, t), F32)
    for s in tiles:
        pos = run_start[s] + before[s]
        exps = [jnp.exp(v - vals[s][0]) for v in vals[s]]
        den = exps[0] + exps[1] + exps[2] + exps[3]
        route_t = jnp.concatenate(
            [blank] + [e / den for e in exps]
            + [jnp.sum(jnp.where(hh, pos, 0.0), axis=0, keepdims=True) for hh in hots[s]] + [blank], axis=0)
        routet_ref[s] = route_t
        fill = jnp.zeros((LANES - route_t.shape[0], LANES), F32)
        for c in range(t // LANES):
            tile = jnp.concatenate([route_t[:, c * LANES:(c + 1) * LANES], fill], axis=0)
            route_ref[s * t + c * LANES:s * t + (c + 1) * LANES, :] = tile.T
        plen_ref[s] = jnp.ceil(counts_row[s][:SUBLANES, :] * (1.0 / RUN_ALIGN)) * RUN_ALIGN


def _outproj_router(y_attn, y_rwkv, x2, w_out_bf16, gain, router_w, router_b):
    n, d = x2.shape
    t = ROUTE_TILE
    nt = n // t
    ns = ROUTER_TILES_PER_STEP if nt % ROUTER_TILES_PER_STEP == 0 else 1
    tm = ns * t
    half = y_attn.shape[1]
    rw_hi = router_w.astype(BF16)
    rwt = jnp.concatenate([rw_hi.T, (router_w - rw_hi.astype(F32)).astype(BF16).T], axis=0)
    rbt = jnp.broadcast_to(router_b.astype(F32)[:, None], (N_EXPERTS, t))
    full = lambda shape: pl.BlockSpec(shape, lambda i: (0,) * len(shape))
    return pl.pallas_call(
        _router_kernel,
        grid=(nt // ns,),
        in_specs=[
            pl.BlockSpec((tm, half), lambda i: (i, 0)),
            pl.BlockSpec((tm, half), lambda i: (i, 0)),
            pl.BlockSpec((tm, d), lambda i: (i, 0)),
            full((2 * half, d)), full((1, d)), full((2 * N_EXPERTS, d)), full((N_EXPERTS, t)),
        ],
        out_specs=[
            pl.BlockSpec((tm, d), lambda i: (i, 0)),
            pl.BlockSpec((tm, d), lambda i: (i, 0)),
            pl.BlockSpec((tm, LANES), lambda i: (i, 0)),
            pl.BlockSpec((ns, 4 * TOP_K, t), lambda i: (i, 0, 0)),
            pl.BlockSpec((ns, SUBLANES, LANES), lambda i: (i, 0, 0)),
        ],
        out_shape=[
            jax.ShapeDtypeStruct((n, d), F32),
            jax.ShapeDtypeStruct((n, d), BF16),
            jax.ShapeDtypeStruct((n, LANES), F32),
            jax.ShapeDtypeStruct((nt, 4 * TOP_K, t), F32),
            jax.ShapeDtypeStruct((nt, SUBLANES, LANES), F32),
        ],
        compiler_params=_params(("parallel",)),
        name="outproj_router",
    )(y_attn, y_rwkv, x2, w_out_bf16, gain.reshape(1, d), rwt, rbt)


def _run_layout(plen_f):
    plen = plen_f[:, 0, :N_EXPERTS].astype(I32)
    tot = jnp.sum(plen, axis=0)
    reg = (tot + EXPERT_BLOCK - 1) // EXPERT_BLOCK * EXPERT_BLOCK
    reg_end = jnp.cumsum(reg)
    estart = reg_end - reg
    goff = estart[None, :] + jnp.cumsum(plen, axis=0) - plen
    return plen, goff, estart + tot, reg - tot, reg_end


TAIL_BITS = tuple(1 << b for b in range(EXPERT_BLOCK.bit_length() - 2, RUN_ALIGN.bit_length() - 2, -1))
assert TAIL_BITS[0] == EXPERT_BLOCK // 2 and TAIL_BITS[-1] == RUN_ALIGN


def _start_run_copies(plen_ref, goff_ref, tile, local_ref, hbm_ref, sem, to_hbm):
    def body(e, lo):
        n = plen_ref[tile * N_EXPERTS + e]
        g = goff_ref[tile * N_EXPERTS + e]

        @pl.when(n > 0)
        def _():
            rows = pl.multiple_of(n, RUN_ALIGN)
            loc = local_ref.at[pl.ds(pl.multiple_of(lo, RUN_ALIGN), rows), :]
            glob = hbm_ref.at[pl.ds(pl.multiple_of(g, RUN_ALIGN), rows), :]
            if to_hbm:
                pltpu.make_async_copy(loc, glob, sem).start()
            else:
                pltpu.make_async_copy(glob, loc, sem).start()
        return lo + n
    lax.fori_loop(0, N_EXPERTS, body, 0)


def _wait_rows(total, local_ref, hbm_ref, sem, to_hbm):
    @pl.when(total > 0)
    def _():
        rows = pl.multiple_of(total, RUN_ALIGN)
        loc = local_ref.at[pl.ds(0, rows), :]
        glob = hbm_ref.at[pl.ds(0, rows), :]
        if to_hbm:
            pltpu.make_async_copy(loc, glob, sem).wait()
        else:
            pltpu.make_async_copy(glob, loc, sem).wait()


def _dispatch_kernel(plen_ref, goff_ref, ttot_ref, tstart_ref, tlen_ref, h_ref, routet_ref, xs_ref,
                     buf_ref, zero_ref, sem_ref):
    t = pl.program_id(0)
    nt = pl.num_programs(0)
    slot = t & 1
    other = 1 - slot
    tt = h_ref.shape[0]

    pos_t = routet_ref[2 * TOP_K:3 * TOP_K, :].astype(I32)
    rows = lax.broadcasted_iota(I32, (TILE_ROWS, tt), 0)
    hit = jnp.zeros((TILE_ROWS, tt), F32)
    for k in range(TOP_K):
        hit = jnp.where(rows == pos_t[k:k + 1, :], 1.0, hit)
    buf_ref[slot] = _dot(hit.astype(BF16), h_ref[...])
    _start_run_copies(plen_ref, goff_ref, t, buf_ref.at[slot], xs_ref, sem_ref.at[slot], True)

    @pl.when(t > 0)
    def _():
        _wait_rows(ttot_ref[t - 1], buf_ref.at[other], xs_ref, sem_ref.at[other], True)

    @pl.when(t == nt - 1)
    def _():
        _wait_rows(ttot_ref[t], buf_ref.at[slot], xs_ref, sem_ref.at[slot], True)
        zero_ref[...] = jnp.zeros_like(zero_ref)
        zrows = zero_ref.shape[0]
        for wait in (False, True):
            def body(e, c):
                n = tlen_ref[e]
                g = tstart_ref[e]
                for bit in TAIL_BITS:
                    @pl.when((n & bit) != 0)
                    def _():
                        o = n & (-2 * bit)
                        cp = pltpu.make_async_copy(
                            zero_ref.at[pl.ds(0, bit), :],
                            xs_ref.at[pl.ds(pl.multiple_of(g + o, RUN_ALIGN), bit), :], sem_ref.at[2])
                        if wait:
                            cp.wait()
                        else:
                            cp.start()
                return c
            lax.fori_loop(0, N_EXPERTS, body, 0)

            def slack(i, c):
                g = tstart_ref[N_EXPERTS] + i * zrows
                cp = pltpu.make_async_copy(
                    zero_ref, xs_ref.at[pl.ds(pl.multiple_of(g, RUN_ALIGN), zrows), :], sem_ref.at[2])
                if wait:
                    cp.wait()
                else:
                    cp.start()
                return c
            lax.fori_loop(0, tlen_ref[N_EXPERTS] // zrows, slack, 0)


def _dispatch(h_bf16, route_t, plen, goff, tstart, tlen, total_rows):
    n, d = h_bf16.shape
    t = ROUTE_TILE
    grid_spec = pltpu.PrefetchScalarGridSpec(
        num_scalar_prefetch=5,
        grid=(n // t,),
        in_specs=[
            pl.BlockSpec((t, d), lambda i, *_: (i, 0)),
            pl.BlockSpec((None, 4 * TOP_K, t), lambda i, *_: (i, 0, 0)),
        ],
        out_specs=pl.BlockSpec(memory_space=pl.ANY),
        scratch_shapes=[
            pltpu.VMEM((2, TILE_ROWS, d), F32),
            pltpu.VMEM((EXPERT_BLOCK // 2, d), F32),
            pltpu.SemaphoreType.DMA((3,)),
        ],
    )
    return pl.pallas_call(
        _dispatch_kernel,
        grid_spec=grid_spec,
        out_shape=jax.ShapeDtypeStruct((total_rows, d), F32),
        compiler_params=_params(("arbitrary",)),
        name="moe_dispatch",
    )(plen.reshape(-1), goff.reshape(-1), jnp.sum(plen, axis=1), tstart, tlen, h_bf16, route_t)


def _experts_kernel(bexp_ref, nvalid_ref, x_ref, w1_ref, b1_ref, w2_ref, b2_ref, y_ref,
                    w1b_ref, w2b_ref):
    j = pl.program_id(0)
    valid = j < nvalid_ref[0]
    changed = (j == 0) | (bexp_ref[j] != bexp_ref[jnp.maximum(j - 1, 0)])
    f = w2_ref.shape[0]

    @pl.when(valid & changed)
    def _():
        w1b_ref[...] = w1_ref[...].astype(BF16)
        w2b_ref[...] = w2_ref[...].astype(BF16)

    @pl.when(valid)
    def _():
        rows = x_ref.shape[0] // EXPERT_SPLIT
        parts = [slice(i * rows, (i + 1) * rows) for i in range(EXPERT_SPLIT)]
        us = [_dot(x_ref[r, :].astype(BF16), w1b_ref[...]) + b1_ref[...] for r in parts]
        acts = []
        for u in us:
            glu = jnp.minimum(u[:, :f], SWIGLU_LIMIT)
            lin = jnp.clip(u[:, f:], -SWIGLU_LIMIT, SWIGLU_LIMIT)
            acts.append((glu * _sigmoid(SWIGLU_ALPHA * glu) * (lin + 1.0)).astype(BF16))
        for r, act in zip(parts, acts):
            y_ref[r, :] = _dot(act, w2b_ref[...]) + b2_ref[...]

    @pl.when(jnp.logical_not(valid))
    def _():
        y_ref[...] = jnp.zeros_like(y_ref)


def _experts(xs, block_expert, nvalid, w1, b1, w2, b2):
    rows, d = xs.shape
    bm = EXPERT_BLOCK
    f2 = w1.shape[2]
    f = w2.shape[1]
    grid_spec = pltpu.PrefetchScalarGridSpec(
        num_scalar_prefetch=2,
        grid=(rows // bm,),
        in_specs=[
            pl.BlockSpec((bm, d), lambda j, be, nv: (jnp.minimum(j, jnp.maximum(nv[0] - 1, 0)), 0)),
            pl.BlockSpec((None, d, f2), lambda j, be, nv: (be[j], 0, 0)),
            pl.BlockSpec((None, 1, f2), lambda j, be, nv: (be[j], 0, 0)),
            pl.BlockSpec((None, f, d), lambda j, be, nv: (be[j], 0, 0)),
            pl.BlockSpec((None, 1, d), lambda j, be, nv: (be[j], 0, 0)),
        ],
        out_specs=pl.BlockSpec((bm, d), lambda j, be, nv: (j, 0)),
        scratch_shapes=[pltpu.VMEM((d, f2), BF16), pltpu.VMEM((f, d), BF16)],
    )
    e = w1.shape[0]
    return pl.pallas_call(
        _experts_kernel,
        grid_spec=grid_spec,
        out_shape=jax.ShapeDtypeStruct((rows, d), F32),
        compiler_params=_params(("arbitrary",)),
        name="moe_experts",
    )(block_expert, nvalid, xs, w1, b1.reshape(e, 1, f2), w2, b2.reshape(e, 1, d))


def _combine_kernel(plen_ref, goff_ref, ttot_ref, route_ref, x1_ref, g_ref, ys_ref, o_ref,
                    buf_ref, sem_ref):
    t = pl.program_id(0)
    nt = pl.num_programs(0)
    slot = t & 1
    other = 1 - slot

    @pl.when(t == 0)
    def _():
        buf_ref[...] = jnp.zeros_like(buf_ref)
        _start_run_copies(plen_ref, goff_ref, t, buf_ref.at[0], ys_ref, sem_ref.at[0], False)

    @pl.when(t + 1 < nt)
    def _():
        _start_run_copies(plen_ref, goff_ref, t + 1, buf_ref.at[other], ys_ref, sem_ref.at[other], False)

    route = route_ref[...]
    tt = route.shape[0]
    col = lax.broadcasted_iota(I32, (tt, TILE_ROWS), 1)
    wts = jnp.zeros((tt, TILE_ROWS), F32)
    for k in range(TOP_K):
        pos_k = route[:, 2 * TOP_K + k:2 * TOP_K + k + 1].astype(I32)
        wts = jnp.where(col == pos_k, route[:, TOP_K + k:TOP_K + k + 1], wts)

    _wait_rows(ttot_ref[t], buf_ref.at[slot], ys_ref, sem_ref.at[slot], False)
    x2 = x1_ref[...] + _dot(wts.astype(BF16), buf_ref[slot].astype(BF16))
    ms = jnp.mean(x2 * x2, axis=-1, keepdims=True)
    o_ref[...] = x2 * lax.rsqrt(ms + NORM_EPS) * g_ref[...]


def _combine(ys, route, x1, gain, plen, goff):
    n, d = x1.shape
    t = ROUTE_TILE
    grid_spec = pltpu.PrefetchScalarGridSpec(
        num_scalar_prefetch=3,
        grid=(n // t,),
        in_specs=[
            pl.BlockSpec((t, LANES), lambda i, *_: (i, 0)),
            pl.BlockSpec((t, d), lambda i, *_: (i, 0)),
            pl.BlockSpec((1, d), lambda i, *_: (0, 0)),
            pl.BlockSpec(memory_space=pl.ANY),
        ],
        out_specs=pl.BlockSpec((t, d), lambda i, *_: (i, 0)),
        scratch_shapes=[
            pltpu.VMEM((2, TILE_ROWS, d), F32),
            pltpu.SemaphoreType.DMA((2,)),
        ],
    )
    return pl.pallas_call(
        _combine_kernel,
        grid_spec=grid_spec,
        out_shape=jax.ShapeDtypeStruct((n, d), F32),
        compiler_params=_params(("arbitrary",)),
        name="moe_combine",
    )(plen.reshape(-1), goff.reshape(-1), jnp.sum(plen, axis=1), route, x1, gain.reshape(1, d), ys)


def _moe_rows(n):
    nt = n // ROUTE_TILE
    worst = TOP_K * n + nt * N_EXPERTS * (RUN_ALIGN - 1) + N_EXPERTS * (EXPERT_BLOCK - RUN_ALIGN)
    return -(-worst // EXPERT_BLOCK) * EXPERT_BLOCK


def _moe_and_final_norm(h_bf16, route, route_t, plen_f, x1, w1, b1, w2, b2, final_gain):
    n = x1.shape[0]
    rows = _moe_rows(n)
    plen, goff, tstart, tlen, reg_end = _run_layout(plen_f)
    nblocks = rows // EXPERT_BLOCK
    starts = jnp.arange(nblocks, dtype=I32) * EXPERT_BLOCK
    nvalid = (reg_end[-1] // EXPERT_BLOCK).astype(I32).reshape(1)
    bexp = jnp.sum((reg_end[None, :] <= starts[:, None]).astype(I32), axis=1)
    last = jnp.take(bexp, jnp.maximum(nvalid[0] - 1, 0))
    bexp = jnp.where(starts < reg_end[-1], bexp, last)
    tstart = jnp.concatenate([tstart, reg_end[-1:]]).astype(I32)
    tlen = jnp.concatenate([tlen, rows - reg_end[-1:]]).astype(I32)
    xs = _dispatch(h_bf16, route_t, plen, goff, tstart, tlen, rows)
    ys = _experts(xs, bexp, nvalid, w1, b1, w2, b2)
    return _combine(ys, route, x1, final_gain, plen, goff)


def kernel(x, norm1_gain, w_in, shift_mu, decay_up, decay_bias, iclr_up, iclr_bias, gate_up, k_k, k_a, r_k, lnx_w, lnx_b, attn_out_gain, w_out, norm2_gain, router_w, router_b, expert_w1, expert_b1, expert_w2, expert_b2, final_norm_gain):
    b, s, d = x.shape
    x2 = x.reshape(b * s, d)
    z_attn, z_rwkv = _inproj(x2, norm1_gain[0], w_in[0].astype(BF16))
    y_attn = _attention(z_attn, attn_out_gain[0], b, s)
    y_rwkv = _rwkv(z_rwkv, b, s, shift_mu[0], decay_up[0], decay_bias[0], iclr_up[0], iclr_bias[0],
                   gate_up[0], k_k[0], k_a[0], r_k[0], lnx_w[0], lnx_b[0])
    x1, h2, route, route_t, plen_f = _outproj_router(
        y_attn, y_rwkv, x2, w_out[0].astype(BF16), norm2_gain[0], router_w[0], router_b[0])
    out = _moe_and_final_norm(h2, route, route_t, plen_f, x1, expert_w1[0], expert_b1[0],
                              expert_w2[0], expert_b2[0], final_norm_gain)
    return out.reshape(b, s, d)
```

```python
import jax
import jax.numpy as jnp
from jax import lax
from jax.experimental import pallas as pl
from jax.experimental.pallas import tpu as pltpu

F32 = jnp.float32
BF16 = jnp.bfloat16
I32 = jnp.int32

D_MODEL = 1024
HEADS = 8
HEAD_DIM = 64
ATTN_WIDTH = HEADS * HEAD_DIM
RWKV_WIDTH = HEADS * HEAD_DIM
DECAY_LORA = 64
ICLR_LORA = 64
GATE_LORA = 128
RWKV_IN = 3 * RWKV_WIDTH + DECAY_LORA + ICLR_LORA + GATE_LORA
IN_WIDTH = 3 * ATTN_WIDTH + RWKV_IN
N_EXPERTS = 32
TOP_K = 4
SWIGLU_ALPHA = 1.702
SWIGLU_LIMIT = 7.0
NORM_EPS = 1e-5
GROUPNORM_EPS = 64e-5
L2_EPS = 1e-12

LANES = 128
SUBLANES = 8
VMEM_LIMIT = 56 * 1024 * 1024
NEG_INF = float("-inf")


def _dot(a, b):
    return jnp.dot(a, b, preferred_element_type=F32)


def _dot_nt(a, b):
    return lax.dot_general(a, b, (((1,), (1,)), ((), ())), preferred_element_type=F32)


def _params(sem, vmem=VMEM_LIMIT):
    return pltpu.CompilerParams(dimension_semantics=sem, vmem_limit_bytes=vmem)


def _inproj_kernel(x_ref, g_ref, w_ref, za_ref, zr_ref):
    x = x_ref[...]
    ms = jnp.mean(x * x, axis=-1, keepdims=True)
    h = (x * lax.rsqrt(ms + NORM_EPS) * g_ref[...]).astype(BF16)
    na = za_ref.shape[1]
    za_ref[...] = _dot(h, w_ref[:, :na])
    zr_ref[...] = _dot(h, w_ref[:, na:])


def _inproj(x2, gain, w_in_bf16, tm=512):
    n, d = x2.shape
    na = 3 * ATTN_WIDTH
    return pl.pallas_call(
        _inproj_kernel,
        grid=(n // tm,),
        in_specs=[
            pl.BlockSpec((tm, d), lambda i: (i, 0)),
            pl.BlockSpec((1, d), lambda i: (0, 0)),
            pl.BlockSpec((d, IN_WIDTH), lambda i: (0, 0)),
        ],
        out_specs=[
            pl.BlockSpec((tm, na), lambda i: (i, 0)),
            pl.BlockSpec((tm, RWKV_IN), lambda i: (i, 0)),
        ],
        out_shape=[
            jax.ShapeDtypeStruct((n, na), F32),
            jax.ShapeDtypeStruct((n, RWKV_IN), F32),
        ],
        compiler_params=_params(("parallel",)),
        name="inproj",
    )(x2, gain.reshape(1, d), w_in_bf16)


ATT_BLK = 128
ATT_DIL = 16
ATT_BLOCKS_PER_ITER = 16
ATT_CLASSES_PER_ITER = 4
ATT_CLASSES_PER_GROUP = 4
LOG2_E = 1.4426950408889634


def _attn_kernel(q_ref, k_ref, v_ref, g_ref, o_ref,
                 kp_ref, vp_ref, e1_ref, a1_ref, of_ref):
    seq = q_ref.shape[0]
    nblk = seq // ATT_BLK
    scale = HEAD_DIM ** -0.5 * LOG2_E
    lane = lax.broadcasted_iota(I32, (1, LANES), 1)
    head_masks = (lane < HEAD_DIM, lane >= HEAD_DIM)

    for r in range(ATT_DIL):
        pb = ((r % 4) * 4 + r // 4) * ATT_BLK
        kp_ref[pb:pb + ATT_BLK, :] = k_ref[pl.ds(r, ATT_BLK, stride=ATT_DIL), :].astype(BF16)
        vp_ref[pb:pb + ATT_BLK, :] = v_ref[pl.ds(r, ATT_BLK, stride=ATT_DIL), :].astype(BF16)

    ii = lax.broadcasted_iota(I32, (ATT_BLK, ATT_BLK), 0)
    jj = lax.broadcasted_iota(I32, (ATT_BLK, ATT_BLK), 1)

    def pack_heads(vals):
        return jnp.where(head_masks[0], vals[0], vals[1])

    cur_ok = jj <= ii
    prev_band = jj >= ii

    def blk_group(g, carry):
        base = g * ATT_BLOCKS_PER_ITER
        rows = [pl.ds(pl.multiple_of((base + u) * ATT_BLK, ATT_BLK), ATT_BLK)
                for u in range(ATT_BLOCKS_PER_ITER)]
        p0 = pl.ds(pl.multiple_of(jnp.maximum(base - 1, 0) * ATT_BLK, ATT_BLK), ATT_BLK)
        qs = [q_ref[rw, :] * scale for rw in rows]
        ks = [k_ref[p0, :].astype(BF16)] + [k_ref[rw, :].astype(BF16) for rw in rows]
        vs = [v_ref[p0, :].astype(BF16)] + [v_ref[rw, :].astype(BF16) for rw in rows]
        prev_ok = [prev_band & (g > 0)] + [prev_band] * (ATT_BLOCKS_PER_ITER - 1)
        chains = [(u, h) for u in range(ATT_BLOCKS_PER_ITER) for h in range(2)]
        qh = [jnp.where(head_masks[h], qs[u], 0.0).astype(BF16) for u, h in chains]
        sc = [jnp.where(cur_ok, _dot_nt(q, ks[u + 1]), NEG_INF) for q, (u, h) in zip(qh, chains)]
        sp = [jnp.where(prev_ok[u], _dot_nt(q, ks[u]), NEG_INF) for q, (u, h) in zip(qh, chains)]
        m = [jnp.max(jnp.maximum(a, b), axis=-1, keepdims=True) for a, b in zip(sc, sp)]
        pc = [jnp.exp2(a - mm) for a, mm in zip(sc, m)]
        pp = [jnp.exp2(b - mm) for b, mm in zip(sp, m)]
        ls = [jnp.sum(a + b, axis=-1, keepdims=True) for a, b in zip(pc, pp)]
        acc = [_dot(a.astype(BF16), vs[u + 1]) + _dot(b.astype(BF16), vs[u])
               for a, b, (u, h) in zip(pc, pp, chains)]
        for u, rw in enumerate(rows):
            den1 = pack_heads(ls[2 * u:2 * u + 2])
            e1_ref[rw, :] = pack_heads(m[2 * u:2 * u + 2]) + jnp.log2(den1)
            a1_ref[rw, :] = pack_heads(acc[2 * u:2 * u + 2]) / den1
        return carry

    lax.fori_loop(0, nblk // ATT_BLOCKS_PER_ITER, blk_group, 0)

    nk = 4 * ATT_BLK
    qi = lax.broadcasted_iota(I32, (ATT_BLK, nk), 0)
    kc_i = lax.broadcasted_iota(I32, (ATT_BLK, nk), 1)
    k_hi = kc_i >> 7
    k_pos = kc_i & (ATT_BLK - 1)
    gain = g_ref[...]

    def res_group(g, carry):
        per_hi = 4 // ATT_CLASSES_PER_ITER
        r_hi = g // per_hi
        rs = [4 * r_hi + (g % per_hi) * ATT_CLASSES_PER_ITER + u for u in range(ATT_CLASSES_PER_ITER)]
        rows = [pl.ds(r, ATT_BLK, stride=ATT_DIL) for r in rs]
        qr = [q_ref[rw, :] * scale for rw in rows]
        k0 = [pl.ds(pl.multiple_of((r & 3) * nk, nk), nk) for r in rs]
        kc = [kp_ref[kk, :] for kk in k0]
        vc = [vp_ref[kk, :] for kk in k0]
        delta = ATT_DIL * (qi - k_pos) + 4 * (r_hi - k_hi)
        in4 = (delta >= 0) & (delta <= 4 * ATT_BLK)
        in16 = (k_hi == r_hi) & (delta >= 0)
        bias = jnp.where(in4 & in16, 1.0, jnp.where(in4 | in16, 0.0, NEG_INF))
        for u0 in range(0, ATT_CLASSES_PER_ITER, ATT_CLASSES_PER_GROUP):
            us = range(u0, u0 + ATT_CLASSES_PER_GROUP)
            chains = [(u, h) for u in us for h in range(2)]
            qh = [jnp.where(head_masks[h], qr[u], 0.0).astype(BF16) for u, h in chains]
            s = [_dot_nt(q, kc[u]) + bias for q, (u, h) in zip(qh, chains)]
            m = [jnp.max(a, axis=-1, keepdims=True) for a in s]
            e = [jnp.exp2(a - mm) for a, mm in zip(s, m)]
            ls = [jnp.sum(a, axis=-1, keepdims=True) for a in e]
            acc = [_dot(a.astype(BF16), vc[u]) for a, (u, h) in zip(e, chains)]
            for j, u in enumerate(us):
                rw = rows[u]
                m2, l2, a2 = (pack_heads(x[2 * j:2 * j + 2]) for x in (m, ls, acc))
                e1 = e1_ref[rw, :]
                mx = jnp.maximum(e1, m2)
                w1 = jnp.exp2(e1 - mx)
                w2 = jnp.exp2(m2 - mx)
                den = w1 + l2 * w2
                o = (a1_ref[rw, :] * w1 + a2 * w2) / den
                o2 = o * o
                msq = pack_heads([jnp.sum(jnp.where(hm, o2, 0.0), axis=-1, keepdims=True)
                                  for hm in head_masks])
                of_ref[rw, :] = o * lax.rsqrt(msq * (1.0 / HEAD_DIM) + NORM_EPS) * gain
        return carry

    for g in range(ATT_DIL // ATT_CLASSES_PER_ITER):
        res_group(g, 0)
    o_ref[...] = of_ref[...].astype(o_ref.dtype)


def _attention(z_attn, out_gain, batch, seq):
    assert seq == ATT_BLK * ATT_DIL
    pairs = ATTN_WIDTH // LANES
    z3 = z_attn.reshape(batch, seq, 3 * ATTN_WIDTH)
    blk = lambda off: pl.BlockSpec((None, seq, LANES), lambda b, p: (b, 0, off + p))
    out = pl.pallas_call(
        _attn_kernel,
        grid=(batch, pairs),
        in_specs=[blk(0), blk(pairs), blk(2 * pairs),
                  pl.BlockSpec((1, LANES), lambda b, p: (0, p))],
        out_specs=pl.BlockSpec((None, seq, LANES), lambda b, p: (b, 0, p)),
        out_shape=jax.ShapeDtypeStruct((batch, seq, ATTN_WIDTH), BF16),
        scratch_shapes=[
            pltpu.VMEM((seq, LANES), BF16), pltpu.VMEM((seq, LANES), BF16),
            pltpu.VMEM((seq, LANES), F32), pltpu.VMEM((seq, LANES), F32), pltpu.VMEM((seq, LANES), F32),
        ],
        compiler_params=_params(("parallel", "parallel")),
        name="dilated_attn",
    )(z3, z3, z3, out_gain.reshape(1, ATTN_WIDTH))
    return out.reshape(batch * seq, ATTN_WIDTH)


WKV_CHUNK = 64
WKV_SEQS_PER_STEP = 8
WKV_SEQS_PER_GROUP = 4


def _split3(x):
    h = x.astype(BF16)
    r1 = x - h.astype(F32)
    m = r1.astype(BF16)
    l = (r1 - m.astype(F32)).astype(BF16)
    return h, m, l


def _seg_sum(x, seg):
    xb = x.astype(BF16)
    return jnp.concatenate(
        [_dot(xb[:, p * LANES:(p + 1) * LANES], seg) for p in range(x.shape[1] // LANES)], axis=1)


def _softplus(y):
    return jnp.maximum(y, 0.0) + jnp.log(1.0 + jnp.exp(-jnp.abs(y)))


def _sigmoid(y):
    return 1.0 / (1.0 + jnp.exp(-y))


def _rwkv_kernel(z_ref, mu_ref, wl_ref, db_ref, ib_ref, gu_ref, kk_ref, ka_ref, rk_ref,
                 lnw_ref, lnb_ref, o_ref, st_ref, prev_ref):
    nseq, cs, _ = z_ref.shape
    w = RWKV_WIDTH
    pairs = w // LANES
    n2 = 2 * cs
    assert n2 == LANES

    @pl.when(pl.program_id(1) == 0)
    def _():
        st_ref[...] = jnp.zeros_like(st_ref)
        prev_ref[...] = jnp.zeros_like(prev_ref)

    lane = lax.broadcasted_iota(I32, (1, LANES), 1)
    head_a = lane < HEAD_DIM
    li = lax.broadcasted_iota(I32, (LANES, LANES), 0)
    lj = lax.broadcasted_iota(I32, (LANES, LANES), 1)
    seg = ((li >> 6) == (lj >> 6)).astype(BF16)
    ti = lax.broadcasted_iota(I32, (cs, cs), 0)
    tj = lax.broadcasted_iota(I32, (cs, cs), 1)
    tri = (ti >= tj).astype(BF16)
    strict = li < lj
    incl = li <= lj
    eye = (li == lj).astype(F32)
    row = lax.broadcasted_iota(I32, (cs, 1), 0)

    def stack2(x, dtype=BF16):
        return jnp.concatenate([jnp.where(head_a, x, 0.0), jnp.where(head_a, 0.0, x)], axis=0).astype(dtype)

    seqs = {}
    sl = lambda p: slice(p * LANES, (p + 1) * LANES)
    inv_n = 1.0 / HEAD_DIM

    def prepare(i):
        z = z_ref[i]
        zprev = jnp.where(row == 0, prev_ref[i, 0:1, :], pltpu.roll(z, 1, axis=0))
        prev_ref[i, 0:1, :] = z[cs - 1:cs, :]
        zs = z + (zprev - z) * mu_ref[...]
        r_ = zs[:, 0:w]
        k_ = zs[:, w:2 * w]
        v_ = zs[:, 2 * w:3 * w]
        lora_in = zs[:, 3 * w:3 * w + LANES]
        gate_in = zs[:, 3 * w + LANES:]
        yield
        xl = jnp.where(head_a, jnp.tanh(lora_in), lora_in).astype(BF16)
        pre = _dot(xl, wl_ref[...])
        wlog = -_softplus(-(pre[:, :w] + db_ref[...])) - 0.5
        lw = -jnp.exp(wlog)
        yield
        rate = _sigmoid(pre[:, w:] + ib_ref[...])
        gate = _dot(_sigmoid(gate_in).astype(BF16), gu_ref[...])
        yield
        kk = k_ * kk_ref[...]
        kk = kk / jnp.maximum(jnp.sqrt(_seg_sum(kk * kk, seg)), L2_EPS)
        kp = k_ * (1.0 + (rate - 1.0) * ka_ref[...])
        bn = kk * rate
        yield
        cum = sum(_dot(tri, t) for t in _split3(lw))
        cend = cum[cs - 1:cs, :]
        yield
        e_inv = jnp.exp(-cum)
        e_end = jnp.exp(cend - cum)
        at = -kk * jnp.exp(cum - lw)
        rt = r_ * jnp.exp(cum)
        yield
        seqs[i] = dict(r=r_, v=v_, kp=kp, gate=gate, g_end=jnp.exp(cend), at=at, rt=rt,
                       bt=bn * e_inv, kt=kp * e_inv, bp=bn * e_end, kpe=kp * e_end)

    def finish(i, y):
        q = seqs[i]
        mean = _seg_sum(y, seg) * inv_n
        yield
        yc = y - mean
        var = _seg_sum(yc * yc, seg) * inv_n
        yield
        y = yc * lax.rsqrt(var + GROUPNORM_EPS) * lnw_ref[...] + lnb_ref[...]
        y = y + _seg_sum(q["r"] * q["kp"] * rk_ref[...], seg) * q["v"]
        yield
        o_ref[i] = (y * q["gate"]).astype(o_ref.dtype)

    side = []

    def tick():
        for gen in list(side):
            if next(gen, side) is side:
                side.remove(gen)

    def drain():
        while side:
            tick()

    def run_chains(group):
        chains = [(i, p) for i in group for p in range(pairs)]
        ar = [jnp.concatenate([stack2(seqs[i]["at"][:, sl(p)]), stack2(seqs[i]["rt"][:, sl(p)])], axis=0)
              for i, p in chains]
        bk = [jnp.concatenate([stack2(seqs[i]["bt"][:, sl(p)]), stack2(seqs[i]["kt"][:, sl(p)])], axis=0)
              for i, p in chains]
        g1 = [_dot_nt(b, a) for b, a in zip(bk, ar)]
        tick()
        ab_t = [jnp.where(strict, g[:n2, :n2], 0.0) for g in g1]
        rb_t = [jnp.where(incl, g[:n2, n2:], 0.0).astype(BF16) for g in g1]
        akrk = [jnp.concatenate([jnp.where(strict, g[n2:, :n2], 0.0), jnp.where(incl, g[n2:, n2:], 0.0)],
                                axis=1).astype(BF16) for g in g1]

        tt = [eye + jnp.where((li >> 1) == (lj >> 1), a, 0.0) for a in ab_t]
        s = 2
        while s < cs:
            sh = s.bit_length() - 1
            off = ((li >> (sh + 1)) == (lj >> (sh + 1))) & ((li >> sh) != (lj >> sh))
            tb = [t.astype(BF16) for t in tt]
            ta = [_dot(t, jnp.where(off, a, 0.0).astype(BF16)).astype(BF16) for t, a in zip(tb, ab_t)]
            tick()
            tt = [t + _dot(x, b) for t, x, b in zip(tt, ta, tb)]
            tick()
            s *= 2
        tt = [t.astype(BF16) for t in tt]

        v_t = [stack2(seqs[i]["v"][:, sl(p)], F32).T.astype(BF16) for i, p in chains]
        s0 = [st_ref[i, p] for i, p in chains]
        g2 = [_dot_nt(s.astype(BF16), a) for s, a in zip(s0, ar)]
        tick()
        g3 = [_dot(v, m) for v, m in zip(v_t, akrk)]
        tick()
        u_t = [_dot((a[:, :n2] + b[:, :n2]).astype(BF16), t).astype(BF16) for a, b, t in zip(g2, g3, tt)]
        tick()
        y_t = [a[:, n2:] + b[:, n2:] + _dot(u, m) for a, b, u, m in zip(g2, g3, u_t, rb_t)]
        tick()
        for c, (i, p) in enumerate(chains):
            bkp = jnp.concatenate([stack2(seqs[i]["bp"][:, sl(p)]), stack2(seqs[i]["kpe"][:, sl(p)])], axis=0)
            st_ref[i, p] = s0[c] * seqs[i]["g_end"][:, sl(p)] + _dot(
                jnp.concatenate([u_t[c], v_t[c]], axis=1), bkp)
        ys = [jnp.where(head_a, yt[:cs, :], yt[cs:, :]) for yt in (y.T for y in y_t)]
        return [jnp.concatenate(ys[j * pairs:(j + 1) * pairs], axis=1) for j in range(len(group))]

    groups = [range(g0, min(g0 + WKV_SEQS_PER_GROUP, nseq)) for g0 in range(0, nseq, WKV_SEQS_PER_GROUP)]
    side.extend(prepare(i) for i in groups[0])
    drain()
    for gi, group in enumerate(groups):
        if gi + 1 < len(groups):
            side.extend(prepare(i) for i in groups[gi + 1])
        outs = run_chains(group)
        drain()
        side.extend(finish(i, y) for i, y in zip(group, outs))
    drain()


def _rwkv(z_rwkv, batch, seq, shift_mu, decay_up, decay_bias, iclr_up, iclr_bias, gate_up,
          k_k, k_a, r_k, lnx_w, lnx_b):
    cs = WKV_CHUNK
    w = RWKV_WIDTH
    nb = WKV_SEQS_PER_STEP if batch % WKV_SEQS_PER_STEP == 0 else 1
    z3 = z_rwkv.reshape(batch, seq, RWKV_IN)
    wl = jnp.zeros((LANES, 2 * w), F32)
    wl = wl.at[:DECAY_LORA, :w].set(decay_up).at[DECAY_LORA:, w:].set(iclr_up).astype(BF16)
    row = lambda a: a.reshape(1, -1).astype(F32)
    full = lambda shape: pl.BlockSpec(shape, lambda b, c: (0,) * len(shape))
    out = pl.pallas_call(
        _rwkv_kernel,
        grid=(batch // nb, seq // cs),
        in_specs=[
            pl.BlockSpec((nb, cs, RWKV_IN), lambda b, c: (b, c, 0)),
            full((1, RWKV_IN)), full((LANES, 2 * w)), full((1, w)), full((1, w)),
            full((GATE_LORA, w)), full((1, w)), full((1, w)), full((1, w)), full((1, w)), full((1, w)),
        ],
        out_specs=pl.BlockSpec((nb, cs, w), lambda b, c: (b, c, 0)),
        out_shape=jax.ShapeDtypeStruct((batch, seq, w), BF16),
        scratch_shapes=[
            pltpu.VMEM((nb, w // LANES, LANES, LANES), F32),
            pltpu.VMEM((nb, SUBLANES, RWKV_IN), F32),
        ],
        compiler_params=_params(("parallel", "arbitrary")),
        name="rwkv7",
    )(z3, row(shift_mu), wl, row(decay_bias), row(iclr_bias), gate_up.astype(BF16),
      row(k_k), row(k_a), row(r_k), row(lnx_w), row(lnx_b))
    return out.reshape(batch * seq, w)


ROUTE_TILE = 256
RUN_ALIGN = SUBLANES
TILE_ROWS = 1280
EXPERT_BLOCK = 512
EXPERT_SPLIT = 2
ROUTER_TILES_PER_STEP = 4
assert TILE_ROWS >= TOP_K * ROUTE_TILE + N_EXPERTS * (RUN_ALIGN - 1) and TILE_ROWS % LANES == 0
HIGHEST = lax.Precision.HIGHEST


def _router_kernel(ya_ref, yr_ref, x_ref, wo_ref, g_ref, rwt_ref, rbt_ref,
                   x1_ref, h_ref, route_ref, routet_ref, plen_ref):
    t = ROUTE_TILE
    ne = N_EXPERTS
    half = ya_ref.shape[1]
    tiles = range(x_ref.shape[0] // t)
    rows = [slice(s * t, (s + 1) * t) for s in tiles]
    x1 = [x_ref[r, :] + _dot(ya_ref[r, :], wo_ref[:half, :]) + _dot(yr_ref[r, :], wo_ref[half:, :])
          for r in rows]
    h = []
    for r, v in zip(rows, x1):
        x1_ref[r, :] = v
        ms = jnp.mean(v * v, axis=-1, keepdims=True)
        h.append(v * lax.rsqrt(ms + NORM_EPS) * g_ref[...])
    h_hi = [v.astype(BF16) for v in h]
    for r, v in zip(rows, h_hi):
        h_ref[r, :] = v

    h_lo = [(v - vb.astype(F32)).astype(BF16) for v, vb in zip(h, h_hi)]
    both = [_dot_nt(rwt_ref[...], vb) for vb in h_hi]
    lg = [bt[:ne, :] + bt[ne:, :] + _dot_nt(rwt_ref[:ne, :], vl) + rbt_ref[...]
          for bt, vl in zip(both, h_lo)]
    e_f = lax.broadcasted_iota(I32, (ne, t), 0).astype(F32)
    vals = [[] for _ in tiles]
    hots = [[] for _ in tiles]
    for _ in range(TOP_K):
        for s in tiles:
            m = jnp.max(lg[s], axis=0, keepdims=True)
            idx = jnp.min(jnp.where(lg[s] == m, e_f, float(ne)), axis=0, keepdims=True)
            hot = e_f == idx
            vals[s].append(m)
            hots[s].append(hot)
            lg[s] = jnp.where(hot, NEG_INF, lg[s])

    ti = lax.broadcasted_iota(I32, (t, t), 0)
    tj = lax.broadcasted_iota(I32, (t, t), 1)
    earlier = (ti < tj).astype(BF16)
    ei = lax.broadcasted_iota(I32, (ne, ne), 0)
    ej = lax.broadcasted_iota(I32, (ne, ne), 1)
    lower = (ej < ei).astype(F32)
    multi = [sum(hh.astype(F32) for hh in hots[s]) for s in tiles]
    before = [_dot(mm.astype(BF16), earlier) for mm in multi]
    padded = [jnp.ceil(jnp.sum(mm, axis=1, keepdims=True) * (1.0 / RUN_ALIGN)) * RUN_ALIGN
              for mm in multi]
    run_start = [jnp.dot(lower, jnp.broadcast_to(pd, (ne, LANES)), precision=HIGHEST,
                         preferred_element_type=F32)[:, 0:1] for pd in padded]
    ones = jnp.ones((2 * SUBLANES, t), BF16)
    counts_row = [_dot_nt(ones, jnp.concatenate([mm, jnp.zeros((LANES - ne, t), F32)], axis=0).astype(BF16))
                  for mm in multi]

    blank = jnp.zeros((TOP_K, t), F32)
    for s in tiles:
        pos = run_start[s] + before[s]
        exps = [jnp.exp(v - vals[s][0]) for v in vals[s]]
        den = exps[0] + exps[1] + exps[2] + exps[3]
        route_t = jnp.concatenate(
            [blank] + [e / den for e in exps]
            + [jnp.sum(jnp.where(hh, pos, 0.0), axis=0, keepdims=True) for hh in hots[s]] + [blank], axis=0)
        routet_ref[s] = route_t
        fill = jnp.zeros((LANES - route_t.shape[0], LANES), F32)
        for c in range(t // LANES):
            tile = jnp.concatenate([route_t[:, c * LANES:(c + 1) * LANES], fill], axis=0)
            route_ref[s * t + c * LANES:s * t + (c + 1) * LANES, :] = tile.T
        plen_ref[s] = jnp.ceil(counts_row[s][:SUBLANES, :] * (1.0 / RUN_ALIGN)) * RUN_ALIGN


def _outproj_router(y_attn, y_rwkv, x2, w_out_bf16, gain, router_w, router_b):
    n, d = x2.shape
    t = ROUTE_TILE
    nt = n // t
    ns = ROUTER_TILES_PER_STEP if nt % ROUTER_TILES_PER_STEP == 0 else 1
    tm = ns * t
    half = y_attn.shape[1]
    rw_hi = router_w.astype(BF16)
    rwt = jnp.concatenate([rw_hi.T, (router_w - rw_hi.astype(F32)).astype(BF16).T], axis=0)
    rbt = jnp.broadcast_to(router_b.astype(F32)[:, None], (N_EXPERTS, t))
    full = lambda shape: pl.BlockSpec(shape, lambda i: (0,) * len(shape))
    return pl.pallas_call(
        _router_kernel,
        grid=(nt // ns,),
        in_specs=[
            pl.BlockSpec((tm, half), lambda i: (i, 0)),
            pl.BlockSpec((tm, half), lambda i: (i, 0)),
            pl.BlockSpec((tm, d), lambda i: (i, 0)),
            full((2 * half, d)), full((1, d)), full((2 * N_EXPERTS, d)), full((N_EXPERTS, t)),
        ],
        out_specs=[
            pl.BlockSpec((tm, d), lambda i: (i, 0)),
            pl.BlockSpec((tm, d), lambda i: (i, 0)),
            pl.BlockSpec((tm, LANES), lambda i: (i, 0)),
            pl.BlockSpec((ns, 4 * TOP_K, t), lambda i: (i, 0, 0)),
            pl.BlockSpec((ns, SUBLANES, LANES), lambda i: (i, 0, 0)),
        ],
        out_shape=[
            jax.ShapeDtypeStruct((n, d), F32),
            jax.ShapeDtypeStruct((n, d), BF16),
            jax.ShapeDtypeStruct((n, LANES), F32),
            jax.ShapeDtypeStruct((nt, 4 * TOP_K, t), F32),
            jax.ShapeDtypeStruct((nt, SUBLANES, LANES), F32),
        ],
        compiler_params=_params(("parallel",)),
        name="outproj_router",
    )(y_attn, y_rwkv, x2, w_out_bf16, gain.reshape(1, d), rwt, rbt)


def _run_layout(plen_f):
    plen = plen_f[:, 0, :N_EXPERTS].astype(I32)
    tot = jnp.sum(plen, axis=0)
    reg = (tot + EXPERT_BLOCK - 1) // EXPERT_BLOCK * EXPERT_BLOCK
    reg_end = jnp.cumsum(reg)
    estart = reg_end - reg
    goff = estart[None, :] + jnp.cumsum(plen, axis=0) - plen
    return plen, goff, estart + tot, reg - tot, reg_end


TAIL_BITS = tuple(1 << b for b in range(EXPERT_BLOCK.bit_length() - 2, RUN_ALIGN.bit_length() - 2, -1))
assert TAIL_BITS[0] == EXPERT_BLOCK // 2 and TAIL_BITS[-1] == RUN_ALIGN


def _start_run_copies(plen_ref, goff_ref, tile, local_ref, hbm_ref, sem, to_hbm):
    def body(e, lo):
        n = plen_ref[tile * N_EXPERTS + e]
        g = goff_ref[tile * N_EXPERTS + e]

        @pl.when(n > 0)
        def _():
            rows = pl.multiple_of(n, RUN_ALIGN)
            loc = local_ref.at[pl.ds(pl.multiple_of(lo, RUN_ALIGN), rows), :]
            glob = hbm_ref.at[pl.ds(pl.multiple_of(g, RUN_ALIGN), rows), :]
            if to_hbm:
                pltpu.make_async_copy(loc, glob, sem).start()
            else:
                pltpu.make_async_copy(glob, loc, sem).start()
        return lo + n
    lax.fori_loop(0, N_EXPERTS, body, 0)


def _wait_rows(total, local_ref, hbm_ref, sem, to_hbm):
    @pl.when(total > 0)
    def _():
        rows = pl.multiple_of(total, RUN_ALIGN)
        loc = local_ref.at[pl.ds(0, rows), :]
        glob = hbm_ref.at[pl.ds(0, rows), :]
        if to_hbm:
            pltpu.make_async_copy(loc, glob, sem).wait()
        else:
            pltpu.make_async_copy(glob, loc, sem).wait()


def _dispatch_kernel(plen_ref, goff_ref, ttot_ref, tstart_ref, tlen_ref, h_ref, routet_ref, xs_ref,
                     buf_ref, zero_ref, sem_ref):
    t = pl.program_id(0)
    nt = pl.num_programs(0)
    slot = t & 1
    other = 1 - slot
    tt = h_ref.shape[0]

    pos_t = routet_ref[2 * TOP_K:3 * TOP_K, :].astype(I32)
    rows = lax.broadcasted_iota(I32, (TILE_ROWS, tt), 0)
    hit = jnp.zeros((TILE_ROWS, tt), F32)
    for k in range(TOP_K):
        hit = jnp.where(rows == pos_t[k:k + 1, :], 1.0, hit)
    buf_ref[slot] = _dot(hit.astype(BF16), h_ref[...])
    _start_run_copies(plen_ref, goff_ref, t, buf_ref.at[slot], xs_ref, sem_ref.at[slot], True)

    @pl.when(t > 0)
    def _():
        _wait_rows(ttot_ref[t - 1], buf_ref.at[other], xs_ref, sem_ref.at[other], True)

    @pl.when(t == nt - 1)
    def _():
        _wait_rows(ttot_ref[t], buf_ref.at[slot], xs_ref, sem_ref.at[slot], True)
        zero_ref[...] = jnp.zeros_like(zero_ref)
        zrows = zero_ref.shape[0]
        for wait in (False, True):
            def body(e, c):
                n = tlen_ref[e]
                g = tstart_ref[e]
                for bit in TAIL_BITS:
                    @pl.when((n & bit) != 0)
                    def _():
                        o = n & (-2 * bit)
                        cp = pltpu.make_async_copy(
                            zero_ref.at[pl.ds(0, bit), :],
                            xs_ref.at[pl.ds(pl.multiple_of(g + o, RUN_ALIGN), bit), :], sem_ref.at[2])
                        if wait:
                            cp.wait()
                        else:
                            cp.start()
                return c
            lax.fori_loop(0, N_EXPERTS, body, 0)

            def slack(i, c):
                g = tstart_ref[N_EXPERTS] + i * zrows
                cp = pltpu.make_async_copy(
                    zero_ref, xs_ref.at[pl.ds(pl.multiple_of(g, RUN_ALIGN), zrows), :], sem_ref.at[2])
                if wait:
                    cp.wait()
                else:
                    cp.start()
                return c
            lax.fori_loop(0, tlen_ref[N_EXPERTS] // zrows, slack, 0)


def _dispatch(h_bf16, route_t, plen, goff, tstart, tlen, total_rows):
    n, d = h_bf16.shape
    t = ROUTE_TILE
    grid_spec = pltpu.PrefetchScalarGridSpec(
        num_scalar_prefetch=5,
        grid=(n // t,),
        in_specs=[
            pl.BlockSpec((t, d), lambda i, *_: (i, 0)),
            pl.BlockSpec((None, 4 * TOP_K, t), lambda i, *_: (i, 0, 0)),
        ],
        out_specs=pl.BlockSpec(memory_space=pl.ANY),
        scratch_shapes=[
            pltpu.VMEM((2, TILE_ROWS, d), F32),
            pltpu.VMEM((EXPERT_BLOCK // 2, d), F32),
            pltpu.SemaphoreType.DMA((3,)),
        ],
    )
    return pl.pallas_call(
        _dispatch_kernel,
        grid_spec=grid_spec,
        out_shape=jax.ShapeDtypeStruct((total_rows, d), F32),
        compiler_params=_params(("arbitrary",)),
        name="moe_dispatch",
    )(plen.reshape(-1), goff.reshape(-1), jnp.sum(plen, axis=1), tstart, tlen, h_bf16, route_t)


def _experts_kernel(bexp_ref, nvalid_ref, x_ref, w1_ref, b1_ref, w2_ref, b2_ref, y_ref,
                    w1b_ref, w2b_ref):
    j = pl.program_id(0)
    valid = j < nvalid_ref[0]
    changed = (j == 0) | (bexp_ref[j] != bexp_ref[jnp.maximum(j - 1, 0)])
    f = w2_ref.shape[0]

    @pl.when(valid & changed)
    def _():
        w1b_ref[...] = w1_ref[...].astype(BF16)
        w2b_ref[...] = w2_ref[...].astype(BF16)

    @pl.when(valid)
    def _():
        rows = x_ref.shape[0] // EXPERT_SPLIT
        parts = [slice(i * rows, (i + 1) * rows) for i in range(EXPERT_SPLIT)]
        us = [_dot(x_ref[r, :].astype(BF16), w1b_ref[...]) + b1_ref[...] for r in parts]
        acts = []
        for u in us:
            glu = jnp.minimum(u[:, :f], SWIGLU_LIMIT)
            lin = jnp.clip(u[:, f:], -SWIGLU_LIMIT, SWIGLU_LIMIT)
            acts.append((glu * _sigmoid(SWIGLU_ALPHA * glu) * (lin + 1.0)).astype(BF16))
        for r, act in zip(parts, acts):
            y_ref[r, :] = _dot(act, w2b_ref[...]) + b2_ref[...]

    @pl.when(jnp.logical_not(valid))
    def _():
        y_ref[...] = jnp.zeros_like(y_ref)


def _experts(xs, block_expert, nvalid, w1, b1, w2, b2):
    rows, d = xs.shape
    bm = EXPERT_BLOCK
    f2 = w1.shape[2]
    f = w2.shape[1]
    grid_spec = pltpu.PrefetchScalarGridSpec(
        num_scalar_prefetch=2,
        grid=(rows // bm,),
        in_specs=[
            pl.BlockSpec((bm, d), lambda j, be, nv: (jnp.minimum(j, jnp.maximum(nv[0] - 1, 0)), 0)),
            pl.BlockSpec((None, d, f2), lambda j, be, nv: (be[j], 0, 0)),
            pl.BlockSpec((None, 1, f2), lambda j, be, nv: (be[j], 0, 0)),
            pl.BlockSpec((None, f, d), lambda j, be, nv: (be[j], 0, 0)),
            pl.BlockSpec((None, 1, d), lambda j, be, nv: (be[j], 0, 0)),
        ],
        out_specs=pl.BlockSpec((bm, d), lambda j, be, nv: (j, 0)),
        scratch_shapes=[pltpu.VMEM((d, f2), BF16), pltpu.VMEM((f, d), BF16)],
    )
    e = w1.shape[0]
    return pl.pallas_call(
        _experts_kernel,
        grid_spec=grid_spec,
        out_shape=jax.ShapeDtypeStruct((rows, d), F32),
        compiler_params=_params(("arbitrary",)),
        name="moe_experts",
    )(block_expert, nvalid, xs, w1, b1.reshape(e, 1, f2), w2, b2.reshape(e, 1, d))


def _combine_kernel(plen_ref, goff_ref, ttot_ref, route_ref, x1_ref, g_ref, ys_ref, o_ref,
                    buf_ref, sem_ref):
    t = pl.program_id(0)
    nt = pl.num_programs(0)
    slot = t & 1
    other = 1 - slot

    @pl.when(t == 0)
    def _():
        buf_ref[...] = jnp.zeros_like(buf_ref)
        _start_run_copies(plen_ref, goff_ref, t, buf_ref.at[0], ys_ref, sem_ref.at[0], False)

    @pl.when(t + 1 < nt)
    def _():
        _start_run_copies(plen_ref, goff_ref, t + 1, buf_ref.at[other], ys_ref, sem_ref.at[other], False)

    route = route_ref[...]
    tt = route.shape[0]
    col = lax.broadcasted_iota(I32, (tt, TILE_ROWS), 1)
    wts = jnp.zeros((tt, TILE_ROWS), F32)
    for k in range(TOP_K):
        pos_k = route[:, 2 * TOP_K + k:2 * TOP_K + k + 1].astype(I32)
        wts = jnp.where(col == pos_k, route[:, TOP_K + k:TOP_K + k + 1], wts)

    _wait_rows(ttot_ref[t], buf_ref.at[slot], ys_ref, sem_ref.at[slot], False)
    x2 = x1_ref[...] + _dot(wts.astype(BF16), buf_ref[slot].astype(BF16))
    ms = jnp.mean(x2 * x2, axis=-1, keepdims=True)
    o_ref[...] = x2 * lax.rsqrt(ms + NORM_EPS) * g_ref[...]


def _combine(ys, route, x1, gain, plen, goff):
    n, d = x1.shape
    t = ROUTE_TILE
    grid_spec = pltpu.PrefetchScalarGridSpec(
        num_scalar_prefetch=3,
        grid=(n // t,),
        in_specs=[
            pl.BlockSpec((t, LANES), lambda i, *_: (i, 0)),
            pl.BlockSpec((t, d), lambda i, *_: (i, 0)),
            pl.BlockSpec((1, d), lambda i, *_: (0, 0)),
            pl.BlockSpec(memory_space=pl.ANY),
        ],
        out_specs=pl.BlockSpec((t, d), lambda i, *_: (i, 0)),
        scratch_shapes=[
            pltpu.VMEM((2, TILE_ROWS, d), F32),
            pltpu.SemaphoreType.DMA((2,)),
        ],
    )
    return pl.pallas_call(
        _combine_kernel,
        grid_spec=grid_spec,
        out_shape=jax.ShapeDtypeStruct((n, d), F32),
        compiler_params=_params(("arbitrary",)),
        name="moe_combine",
    )(plen.reshape(-1), goff.reshape(-1), jnp.sum(plen, axis=1), route, x1, gain.reshape(1, d), ys)


def _moe_rows(n):
    nt = n // ROUTE_TILE
    worst = TOP_K * n + nt * N_EXPERTS * (RUN_ALIGN - 1) + N_EXPERTS * (EXPERT_BLOCK - RUN_ALIGN)
    return -(-worst // EXPERT_BLOCK) * EXPERT_BLOCK


def _moe_and_final_norm(h_bf16, route, route_t, plen_f, x1, w1, b1, w2, b2, final_gain):
    n = x1.shape[0]
    rows = _moe_rows(n)
    plen, goff, tstart, tlen, reg_end = _run_layout(plen_f)
    nblocks = rows // EXPERT_BLOCK
    starts = jnp.arange(nblocks, dtype=I32) * EXPERT_BLOCK
    nvalid = (reg_end[-1] // EXPERT_BLOCK).astype(I32).reshape(1)
    bexp = jnp.sum((reg_end[None, :] <= starts[:, None]).astype(I32), axis=1)
    last = jnp.take(bexp, jnp.maximum(nvalid[0] - 1, 0))
    bexp = jnp.where(starts < reg_end[-1], bexp, last)
    tstart = jnp.concatenate([tstart, reg_end[-1:]]).astype(I32)
    tlen = jnp.concatenate([tlen, rows - reg_end[-1:]]).astype(I32)
    xs = _dispatch(h_bf16, route_t, plen, goff, tstart, tlen, rows)
    ys = _experts(xs, bexp, nvalid, w1, b1, w2, b2)
    return _combine(ys, route, x1, final_gain, plen, goff)


def kernel(x, norm1_gain, w_in, shift_mu, decay_up, decay_bias, iclr_up, iclr_bias, gate_up, k_k, k_a, r_k, lnx_w, lnx_b, attn_out_gain, w_out, norm2_gain, router_w, router_b, expert_w1, expert_b1, expert_w2, expert_b2, final_norm_gain):
    b, s, d = x.shape
    x2 = x.reshape(b * s, d)
    z_attn, z_rwkv = _inproj(x2, norm1_gain[0], w_in[0].astype(BF16))
    y_attn = _attention(z_attn, attn_out_gain[0], b, s)
    y_rwkv = _rwkv(z_rwkv, b, s, shift_mu[0], decay_up[0], decay_bias[0], iclr_up[0], iclr_bias[0],
                   gate_up[0], k_k[0], k_a[0], r_k[0], lnx_w[0], lnx_b[0])
    x1, h2, route, route_t, plen_f = _outproj_router(
        y_attn, y_rwkv, x2, w_out[0].astype(BF16), norm2_gain[0], router_w[0], router_b[0])
    out = _moe_and_final_norm(h2, route, route_t, plen_f, x1, expert_w1[0], expert_b1[0],
                              expert_w2[0], expert_b2[0], final_norm_gain)
    return out.reshape(b, s, d)
```
